```python
import jax, jax.numpy as jnp
from jax import lax
import numpy as np

D_MODEL = 2048
BATCH = 16
SEQ = 2048
DEPTH = 1
DEC_BATCH = 32
DEC_SEQ = 64
PAST_LEN = 2048

CHUNK = 64
D_CONV = 1024
CONV_WIDTH = 31
N_HEADS = 16
HEAD_DIM = 128
D_ATTN = N_HEADS * HEAD_DIM
IDX_HEADS = 16
IDX_DIM = 64
TOPK_MAX = 256
Q_BLOCK = 128
D_FF = -(-8 * D_MODEL // (3 * 256)) * 256
EPS = 1e-6

SPLITS = [D_CONV, D_CONV, D_ATTN, D_ATTN, D_ATTN, IDX_HEADS * IDX_DIM, IDX_DIM, IDX_HEADS, D_MODEL, D_MODEL]
D_IN = sum(SPLITS)
SPLIT_POINTS = [int(p) for p in np.cumsum(SPLITS)[:-1]]

kernel_name = "chunk_causal_conformer_dsa_hybrid_step"


def _rms_norm(x, g):
    xf = x.astype(jnp.float32)
    y = xf * lax.rsqrt(jnp.mean(xf * xf, axis=-1, keepdims=True) + EPS)
    return (y * g.astype(jnp.float32)).astype(x.dtype)


def _layer_norm(x, g, b):
    xf = x.astype(jnp.float32)
    mu = jnp.mean(xf, axis=-1, keepdims=True)
    var = jnp.mean(jnp.square(xf - mu), axis=-1, keepdims=True)
    y = (xf - mu) * lax.rsqrt(var + EPS)
    return (y * g.astype(jnp.float32) + b.astype(jnp.float32)).astype(x.dtype)


def _pre_mix(x, g_mix, w_in, idx_ln_g, idx_ln_b):
    B, T, _ = x.shape
    h = _rms_norm(x, g_mix)
    z = h @ w_in
    a, b, q, k, v, qi, ki, wi, ga, gb = jnp.split(z, SPLIT_POINTS, axis=-1)
    u = a * jax.nn.sigmoid(b)
    q = q.reshape(B, T, N_HEADS, HEAD_DIM)
    k = k.reshape(B, T, N_HEADS, HEAD_DIM)
    v = v.reshape(B, T, N_HEADS, HEAD_DIM)
    qi = qi.reshape(B, T, IDX_HEADS, IDX_DIM)
    ki = _layer_norm(ki, idx_ln_g, idx_ln_b)
    return u, q, k, v, qi, ki, wi, ga, gb


def _depthwise_causal_conv(u_padded, conv_w, conv_b):
    y = lax.conv_general_dilated(
        u_padded, conv_w[:, None, :], window_strides=(1,), padding='VALID',
        dimension_numbers=('NWC', 'WIO', 'NWC'), feature_group_count=D_CONV)
    return y + conv_b


def _sparse_attend(q, qi, wi, qpos, k, v, ki, topk):
    L = k.shape[0]
    kpos = jnp.arange(L, dtype=jnp.int32)
    limit = (qpos // CHUNK + 1) * CHUNK
    admissible = kpos[None, :] < limit[:, None]
    logits = jnp.einsum('qhd,ld->qhl', qi.astype(jnp.float32), ki.astype(jnp.float32)) * (IDX_DIM ** -0.5)
    score = jnp.einsum('qh,qhl->ql', wi.astype(jnp.float32) * (IDX_HEADS ** -0.5), jax.nn.relu(logits))
    score = jnp.where(admissible, score, -jnp.inf)
    _, sel = lax.top_k(score, topk)
    valid = jnp.take_along_axis(admissible, sel, axis=1)
    kg = k[sel]
    vg = v[sel]
    s = jnp.einsum('qhd,qkhd->qhk', q, kg, preferred_element_type=jnp.float32) * (HEAD_DIM ** -0.5)
    s = jnp.where(valid[:, None, :], s, -jnp.inf)
    p = jax.nn.softmax(s, axis=-1)
    o = jnp.einsum('qhk,qkhd->qhd', p.astype(v.dtype), vg)
    return o.reshape(q.shape[0], D_ATTN)


def _prompt_attention(q, qi, wi, k, v, ki):
    B, S = q.shape[0], q.shape[1]
    nb = S // Q_BLOCK
    topk = min(TOPK_MAX, S // 4)
    fold = lambda t: t.reshape((B * nb, Q_BLOCK) + t.shape[2:])
    b_ids = jnp.repeat(jnp.arange(B, dtype=jnp.int32), nb)
    blk_ids = jnp.tile(jnp.arange(nb, dtype=jnp.int32), B)

    def body(item):
        qb, qib, wib, bi, j = item
        qpos = j * Q_BLOCK + jnp.arange(Q_BLOCK, dtype=jnp.int32)
        return _sparse_attend(qb, qib, wib, qpos, k[bi], v[bi], ki[bi], topk)

    o = lax.map(body, (fold(q), fold(qi), fold(wi), b_ids, blk_ids))
    return o.reshape(B, S, D_ATTN)


def _sample_attention(q, qi, wi, k, v, ki, ck, cv, cki):
    P, T = ck.shape[1], q.shape[1]
    topk = min(TOPK_MAX, (P + T) // 4)
    qpos = P + jnp.arange(T, dtype=jnp.int32)

    def body(item):
        qb, qib, wib, kb, vb, kib, ckb, cvb, ckib = item
        return _sparse_attend(qb, qib, wib, qpos,
                              jnp.concatenate([ckb, kb], axis=0),
                              jnp.concatenate([cvb, vb], axis=0),
                              jnp.concatenate([ckib, kib], axis=0), topk)

    return lax.map(body, (q, qi, wi, k, v, ki, ck, cv, cki))


def _post_mix(x, conv_pre, attn_o, ga, gb, conv_ln_g, conv_ln_b, w_conv_out, w_attn_out,
              w_out, g_ffn, w_gate, w_up, w_down):
    c = jax.nn.silu(_layer_norm(conv_pre, conv_ln_g, conv_ln_b)) @ w_conv_out
    a = attn_o @ w_attn_out
    m = jax.nn.sigmoid(ga) * c + jax.nn.sigmoid(gb) * a
    x = x + m @ w_out
    h = _rms_norm(x, g_ffn)
    return x + (jax.nn.silu(h @ w_gate) * (h @ w_up)) @ w_down


def setup_inputs(seed: int = 0) -> dict:
    key = jax.random.key(seed)
    ks = jax.random.split(key, 32)
    f32 = jnp.float32
    nrm = lambda k, shape, s: jax.random.normal(k, shape, f32) * s
    return {
        'x_prompt': nrm(ks[0], (BATCH, SEQ, D_MODEL), 1.0),
        'x_sample': nrm(ks[1], (DEC_BATCH, DEC_SEQ, D_MODEL), 1.0),
        'cache_conv': nrm(ks[2], (DEPTH, DEC_BATCH, CONV_WIDTH - 1, D_CONV), 0.5),
        'cache_k': nrm(ks[3], (DEPTH, DEC_BATCH, PAST_LEN, N_HEADS, HEAD_DIM), 1.0),
        'cache_v': nrm(ks[4], (DEPTH, DEC_BATCH, PAST_LEN, N_HEADS, HEAD_DIM), 1.0),
        'cache_kidx': nrm(ks[5], (DEPTH, DEC_BATCH, PAST_LEN, IDX_DIM), 1.0),
        'g_mix': 1.0 + nrm(ks[6], (DEPTH, D_MODEL), 0.02),
        'w_in': nrm(ks[7], (DEPTH, D_MODEL, D_IN), D_MODEL ** -0.5),
        'conv_w': nrm(ks[8], (DEPTH, CONV_WIDTH, D_CONV), CONV_WIDTH ** -0.5),
        'conv_b': nrm(ks[9], (DEPTH, D_CONV), 0.01),
        'conv_ln_g': 1.0 + nrm(ks[10], (DEPTH, D_CONV), 0.02),
        'conv_ln_b': nrm(ks[11], (DEPTH, D_CONV), 0.01),
        'w_conv_out': nrm(ks[12], (DEPTH, D_CONV, D_MODEL), D_CONV ** -0.5),
        'idx_ln_g': 1.0 + nrm(ks[13], (DEPTH, IDX_DIM), 0.02),
        'idx_ln_b': nrm(ks[14], (DEPTH, IDX_DIM), 0.01),
        'w_attn_out': nrm(ks[15], (DEPTH, D_ATTN, D_MODEL), D_ATTN ** -0.5),
        'w_out': nrm(ks[16], (DEPTH, D_MODEL, D_MODEL), D_MODEL ** -0.5),
        'g_ffn': 1.0 + nrm(ks[17], (DEPTH, D_MODEL), 0.02),
        'w_gate': nrm(ks[18], (DEPTH, D_MODEL, D_FF), D_MODEL ** -0.5),
        'w_up': nrm(ks[19], (DEPTH, D_MODEL, D_FF), D_MODEL ** -0.5),
        'w_down': nrm(ks[20], (DEPTH, D_FF, D_MODEL), D_FF ** -0.5),
        'g_final': 1.0 + nrm(ks[21], (D_MODEL,), 0.02),
    }


def reference(x_prompt, x_sample, cache_conv, cache_k, cache_v, cache_kidx, g_mix, w_in, conv_w,
              conv_b, conv_ln_g, conv_ln_b, w_conv_out, idx_ln_g, idx_ln_b, w_attn_out, w_out,
              g_ffn, w_gate, w_up, w_down, g_final):
    xp, xs = x_prompt, x_sample
    conv_p, k_p, v_p, kidx_p = [], [], [], []
    conv_s, k_s, v_s, kidx_s = [], [], [], []
    for l in range(DEPTH):
        u, q, k, v, qi, ki, wi, ga, gb = _pre_mix(xp, g_mix[l], w_in[l], idx_ln_g[l], idx_ln_b[l])
        u_pad = jnp.pad(u, ((0, 0), (CONV_WIDTH - 1, 0), (0, 0)))
        conv_pre = _depthwise_causal_conv(u_pad, conv_w[l], conv_b[l])
        attn_o = _prompt_attention(q, qi, wi, k, v, ki)
        xp = _post_mix(xp, conv_pre, attn_o, ga, gb, conv_ln_g[l], conv_ln_b[l], w_conv_out[l],
                       w_attn_out[l], w_out[l], g_ffn[l], w_gate[l], w_up[l], w_down[l])
        conv_p.append(u_pad[:, -(CONV_WIDTH - 1):, :])
        k_p.append(k)
        v_p.append(v)
        kidx_p.append(ki)
        u, q, k, v, qi, ki, wi, ga, gb = _pre_mix(xs, g_mix[l], w_in[l], idx_ln_g[l], idx_ln_b[l])
        u_cat = jnp.concatenate([cache_conv[l].astype(u.dtype), u], axis=1)
        conv_pre = _depthwise_causal_conv(u_cat, conv_w[l], conv_b[l])
        attn_o = _sample_attention(q, qi, wi, k, v, ki, cache_k[l], cache_v[l], cache_kidx[l])
        xs = _post_mix(xs, conv_pre, attn_o, ga, gb, conv_ln_g[l], conv_ln_b[l], w_conv_out[l],
                       w_attn_out[l], w_out[l], g_ffn[l], w_gate[l], w_up[l], w_down[l])
        conv_s.append(u_cat[:, -(CONV_WIDTH - 1):, :])
        k_s.append(k)
        v_s.append(v)
        kidx_s.append(ki)
    y_prompt = _rms_norm(xp, g_final)
    y_sample = _rms_norm(xs, g_final)
    return (y_prompt, y_sample,
            jnp.stack(conv_p), jnp.stack(k_p), jnp.stack(v_p), jnp.stack(kidx_p),
            jnp.stack(conv_s), jnp.stack(k_s), jnp.stack(v_s), jnp.stack(kidx_s))
```

```python
import functools

import jax
import jax.numpy as jnp
from jax import lax
from jax.experimental import pallas as pl
from jax.experimental.pallas import tpu as pltpu

F32 = jnp.float32
BF16 = jnp.bfloat16

CHUNK = 64
CONV_WIDTH = 31
N_HEADS = 16
HEAD_DIM = 128
IDX_HEADS = 16
IDX_DIM = 64
TOPK_MAX = 256
EPS = 1e-6

LANES = 128
CONV_HALO = 32
CONV_ROWS = 32
SEARCH_UNROLL = 4
VMEM_LIMIT = 56 * 2**20


def _params(*sem):
    return pltpu.CompilerParams(dimension_semantics=sem, vmem_limit_bytes=VMEM_LIMIT)


def _rms_kernel(x_ref, g_ref, o_ref):
    x = x_ref[...]
    ms = jnp.mean(x * x, axis=-1, keepdims=True)
    o_ref[...] = (x * lax.rsqrt(ms + EPS) * g_ref[...]).astype(o_ref.dtype)


def _rmsnorm(x, g, out_dtype, tm=512):
    m, d = x.shape
    tm = min(tm, m)
    return pl.pallas_call(
        _rms_kernel,
        grid=(m // tm,),
        in_specs=[pl.BlockSpec((tm, d), lambda i: (i, 0)),
                  pl.BlockSpec((1, d), lambda i: (0, 0))],
        out_specs=pl.BlockSpec((tm, d), lambda i: (i, 0)),
        out_shape=jax.ShapeDtypeStruct((m, d), out_dtype),
        compiler_params=_params("parallel"),
        name="rmsnorm",
    )(x, g.reshape(1, d))


def _proj_kernel(h_ref, w_ref, *o_refs, scale):
    z = jnp.dot(h_ref[...], w_ref[...], preferred_element_type=F32)
    if scale != 1.0:
        z = z * scale
    for o_ref in o_refs:
        o_ref[...] = z.astype(o_ref.dtype)


def _glu_kernel(h_ref, w_ref, o_ref):
    z = jnp.dot(h_ref[...], w_ref[...], preferred_element_type=F32)
    half = z.shape[1] // 2
    o_ref[...] = z[:, :half] * jax.nn.sigmoid(z[:, half:])


def _sigmoid_kernel(h_ref, w_ref, o_ref):
    z = jnp.dot(h_ref[...], w_ref[...], preferred_element_type=F32)
    o_ref[...] = jax.nn.sigmoid(z).astype(o_ref.dtype)


def _idx_small_kernel(h_ref, w_ref, g_ref, b_ref, ki_ref, wi_ref):
    z = jnp.dot(h_ref[...], w_ref[...], preferred_element_type=F32)
    kz = z[:, :IDX_DIM]
    mu = jnp.mean(kz, axis=-1, keepdims=True)
    var = jnp.mean(jnp.square(kz - mu), axis=-1, keepdims=True)
    ki_ref[...] = (kz - mu) * lax.rsqrt(var + EPS) * g_ref[...] + b_ref[...]
    wi_ref[...] = z[:, IDX_DIM:IDX_DIM + IDX_HEADS]


def _matmul(body, h, w, outs, extra=(), tm=1024, tn=1024, name="proj"):
    m, k = h.shape
    n = w.shape[1]
    tm = min(tm, m)
    tn = min(tn, n)
    nt = n // tn
    outs = [(tn if c is None else c, dt) for c, dt in outs]
    in_specs = [pl.BlockSpec((tm, k), lambda i, j: (i, 0)),
                pl.BlockSpec((k, tn), lambda i, j: (0, j))]
    in_specs += [pl.BlockSpec(e.shape, lambda i, j: (0, 0)) for e in extra]
    res = pl.pallas_call(
        body,
        grid=(m // tm, nt),
        in_specs=in_specs,
        out_specs=[pl.BlockSpec((tm, c), lambda i, j: (i, j)) for c, _ in outs],
        out_shape=[jax.ShapeDtypeStruct((m, c * nt), dt) for c, dt in outs],
        compiler_params=_params("parallel", "arbitrary"),
        name=name,
    )(h, w, *extra)
    return res


def _conv_kernel(cur_ref, prev_ref, cache_ref, cw_ref, cb_ref, g_ref, b_ref, o_ref, ext_ref, *, tc):
    c = pl.program_id(1)
    ext_ref[0:CONV_HALO, :] = jnp.where(c == 0, cache_ref[0], prev_ref[0])
    ext_ref[CONV_HALO:CONV_HALO + tc, :] = cur_ref[0]
    shift = CONV_HALO - (CONV_WIDTH - 1)

    for base in range(0, tc, CONV_ROWS):
        acc = jnp.broadcast_to(cb_ref[...], (CONV_ROWS, cb_ref.shape[1]))
        for w in range(CONV_WIDTH):
            acc = acc + ext_ref[base + shift + w:base + shift + w + CONV_ROWS, :] * cw_ref[w:w + 1, :]
        mu = jnp.mean(acc, axis=-1, keepdims=True)
        var = jnp.mean(jnp.square(acc - mu), axis=-1, keepdims=True)
        y = (acc - mu) * lax.rsqrt(var + EPS) * g_ref[...] + b_ref[...]
        o_ref[0, base:base + CONV_ROWS, :] = (y * jax.nn.sigmoid(y)).astype(o_ref.dtype)


def _conv_branch(u, cache_pad, conv_w, conv_b, ln_g, ln_b, tc):
    b, t, c = u.shape
    tc = min(tc, t)
    per = tc // CONV_HALO
    return pl.pallas_call(
        functools.partial(_conv_kernel, tc=tc),
        grid=(b, t // tc),
        in_specs=[pl.BlockSpec((1, tc, c), lambda i, j: (i, j, 0)),
                  pl.BlockSpec((1, CONV_HALO, c), lambda i, j: (i, jnp.maximum(j * per - 1, 0), 0)),
                  pl.BlockSpec((1, CONV_HALO, c), lambda i, j: (i, 0, 0)),
                  pl.BlockSpec((CONV_WIDTH, c), lambda i, j: (0, 0)),
                  pl.BlockSpec((1, c), lambda i, j: (0, 0)),
                  pl.BlockSpec((1, c), lambda i, j: (0, 0)),
                  pl.BlockSpec((1, c), lambda i, j: (0, 0))],
        out_specs=pl.BlockSpec((1, tc, c), lambda i, j: (i, j, 0)),
        out_shape=jax.ShapeDtypeStruct((b, t, c), BF16),
        scratch_shapes=[pltpu.VMEM((CONV_HALO + tc, c), F32)],
        compiler_params=_params("parallel", "arbitrary"),
        name="conv_branch",
    )(u, u, cache_pad, conv_w, conv_b.reshape(1, c), ln_g.reshape(1, c), ln_b.reshape(1, c))


def _select_kernel(qi_ref, wi_ref, klo_ref, khi_ref, bias_ref, score_ref, *, tq, n_keys, qpos0, topk):
    j = pl.program_id(1)
    wi = wi_ref[0] * (IDX_HEADS ** -0.5 * IDX_DIM ** -0.5)
    acc = None
    for p in range(IDX_HEADS // 2):
        qt = qi_ref[0, :, p * LANES:(p + 1) * LANES]
        for head, k_ref in ((2 * p, klo_ref), (2 * p + 1, khi_ref)):
            logits = jnp.dot(qt, k_ref[0], preferred_element_type=F32)
            term = wi[:, head:head + 1] * jnp.maximum(logits, 0.0)
            acc = term if acc is None else acc + term

    kpos = lax.broadcasted_iota(jnp.int32, (tq, n_keys), 1)
    qpos = qpos0 + j * tq + lax.broadcasted_iota(jnp.int32, (tq, 1), 0)
    limit = (qpos // CHUNK + 1) * CHUNK
    adm = kpos < limit
    score_ref[...] = jnp.where(adm, acc, -jnp.inf)
    kf = float(topk)
    n_adm = jnp.minimum(limit, n_keys).astype(F32)
    searching = (n_adm > kf).astype(F32)
    lo0 = jnp.min(jnp.where(adm, acc, jnp.inf), axis=1, keepdims=True)
    hi0 = jnp.max(score_ref[...], axis=1, keepdims=True)
    thr0 = jnp.full((tq, 1), -jnp.inf, F32)
    tie0 = jnp.zeros((tq, 1), F32)

    def count_ge(t):
        return jnp.sum(jnp.where(score_ref[...] >= t, 1.0, 0.0), axis=1, keepdims=True)

    def step(state):
        lo, hi, thr, active, tie = state
        mid = 0.5 * lo + 0.5 * hi
        cnt = count_ge(mid)
        on = active > 0.0
        ge = cnt >= kf
        exact = cnt == kf
        collapsed = jnp.logical_or(mid <= lo, mid >= hi)
        lo = jnp.where(jnp.logical_and(on, ge), mid, lo)
        hi = jnp.where(jnp.logical_and(on, jnp.logical_not(ge)), mid, hi)
        thr = jnp.where(jnp.logical_and(on, exact), mid, thr)
        tie = jnp.where(jnp.logical_and(on, jnp.logical_and(collapsed, jnp.logical_not(exact))), 1.0, tie)
        active = jnp.where(jnp.logical_or(exact, collapsed), 0.0, active)
        return lo, hi, thr, active, tie

    def body(carry):
        state = carry[:5]
        for _ in range(SEARCH_UNROLL):
            state = step(state)
        return state + (jnp.max(state[3]),)

    lo, hi, thr, _, tie, _ = lax.while_loop(
        lambda carry: carry[5] > 0.0, body,
        (lo0, hi0, thr0, searching, tie0, jnp.max(searching)))

    score = score_ref[...]
    bias_ref[0] = jnp.where(jnp.logical_and(score >= thr, adm), 0.0, -jnp.inf).astype(bias_ref.dtype)

    @pl.when(jnp.max(tie) > 0.0)
    def _():
        score = score_ref[...]
        t = jnp.where(count_ge(hi) >= kf, hi, lo)
        gt = score > t
        eq = score == t
        need = kf - jnp.sum(jnp.where(gt, 1.0, 0.0), axis=1, keepdims=True)
        r = lax.broadcasted_iota(jnp.int32, (LANES, LANES), 0)
        cidx = lax.broadcasted_iota(jnp.int32, (LANES, LANES), 1)
        tri = jnp.where(r <= cidx, 1.0, 0.0).astype(BF16)
        carry = jnp.zeros((tq, 1), F32)
        for s in range(n_keys // LANES):
            sl = slice(s * LANES, (s + 1) * LANES)
            e = jnp.where(eq[:, sl], 1.0, 0.0)
            pre = jnp.dot(e.astype(BF16), tri, preferred_element_type=F32) + carry
            carry = pre[:, LANES - 1:LANES]
            keep = jnp.logical_or(gt[:, sl], jnp.logical_and(eq[:, sl], pre <= need))
            plain = jnp.logical_and(score[:, sl] >= thr, adm[:, sl])
            bias = jnp.where(tie > 0.0, jnp.where(keep, 0.0, -jnp.inf), jnp.where(plain, 0.0, -jnp.inf))
            bias_ref[0, :, sl] = bias.astype(bias_ref.dtype)


def _select(qi, wi, klo, khi, tq, qpos0, topk):
    b, t, dq = qi.shape
    n_keys = klo.shape[2]
    return pl.pallas_call(
        functools.partial(_select_kernel, tq=tq, n_keys=n_keys, qpos0=qpos0, topk=topk),
        grid=(b, t // tq),
        in_specs=[pl.BlockSpec((1, tq, dq), lambda i, j: (i, j, 0)),
                  pl.BlockSpec((1, tq, IDX_HEADS), lambda i, j: (i, j, 0)),
                  pl.BlockSpec((1, LANES, n_keys), lambda i, j: (i, 0, 0)),
                  pl.BlockSpec((1, LANES, n_keys), lambda i, j: (i, 0, 0))],
        out_specs=pl.BlockSpec((1, tq, n_keys), lambda i, j: (i, j, 0)),
        out_shape=jax.ShapeDtypeStruct((b, t, n_keys), BF16),
        scratch_shapes=[pltpu.VMEM((tq, n_keys), F32)],
        compiler_params=_params("parallel", "arbitrary"),
        name="select_topk",
    )(qi, wi, klo, khi)


def _attn_kernel(q_ref, bias_ref, *refs, heads, seg_lens):
    nseg = len(seg_lens)
    k_refs, v_refs, o_ref = refs[:nseg], refs[nseg:2 * nseg], refs[2 * nseg]
    offs = [sum(seg_lens[:i]) for i in range(nseg)]
    biases = [bias_ref[0, :, o:o + n].astype(F32) for o, n in zip(offs, seg_lens)]
    for h in range(heads):
        sl = slice(h * HEAD_DIM, (h + 1) * HEAD_DIM)
        qh = q_ref[0, :, sl]
        scores = []
        for k_ref, bias in zip(k_refs, biases):
            kh = k_ref[0, :, sl].astype(BF16)
            scores.append(lax.dot_general(qh, kh, (((1,), (1,)), ((), ())),
                                          preferred_element_type=F32) + bias)
        m = scores[0].max(axis=-1, keepdims=True)
        for s in scores[1:]:
            m = jnp.maximum(m, s.max(axis=-1, keepdims=True))
        denom = None
        out = None
        for s, v_ref in zip(scores, v_refs):
            p = jnp.exp(s - m)
            psum = p.sum(axis=-1, keepdims=True)
            pv = jnp.dot(p.astype(BF16), v_ref[0, :, sl].astype(BF16), preferred_element_type=F32)
            denom = psum if denom is None else denom + psum
            out = pv if out is None else out + pv
        o_ref[0, :, sl] = (out / denom).astype(o_ref.dtype)


def _attention(q, bias, ks, vs, tq, heads_per_step):
    b, t, d = q.shape
    seg_lens = tuple(k.shape[1] for k in ks)
    dh = heads_per_step * HEAD_DIM
    kv_specs = [pl.BlockSpec((1, n, dh), lambda i, j, g: (i, 0, g)) for n in seg_lens]
    return pl.pallas_call(
        functools.partial(_attn_kernel, heads=heads_per_step, seg_lens=seg_lens),
        grid=(b, t // tq, d // dh),
        in_specs=[pl.BlockSpec((1, tq, dh), lambda i, j, g: (i, j, g)),
                  pl.BlockSpec((1, tq, sum(seg_lens)), lambda i, j, g: (i, j, 0))] + kv_specs + kv_specs,
        out_specs=pl.BlockSpec((1, tq, dh), lambda i, j, g: (i, j, g)),
        out_shape=jax.ShapeDtypeStruct((b, t, d), BF16),
        compiler_params=_params("parallel", "arbitrary", "arbitrary"),
        name="sparse_attention",
    )(q, bias, *ks, *vs)


def _postmix_kernel(x_ref, c_ref, a_ref, sg_ref, wc_ref, wa_ref, wo_ref, o_ref):
    d = x_ref.shape[1]
    c = jnp.dot(c_ref[...], wc_ref[...], preferred_element_type=F32)
    a = jnp.dot(a_ref[...], wa_ref[...], preferred_element_type=F32)
    m = sg_ref[:, :d].astype(F32) * c + sg_ref[:, d:].astype(F32) * a
    o_ref[...] = x_ref[...] + jnp.dot(m.astype(BF16), wo_ref[...], preferred_element_type=F32)


def _postmix(x, cact, attn, sg, wc, wa, wo, tm=256):
    m, d = x.shape
    tm = min(tm, m)
    row = lambda width: pl.BlockSpec((tm, width), lambda i: (i, 0))
    full = lambda w: pl.BlockSpec(w.shape, lambda i: (0, 0), pipeline_mode=pl.Buffered(1))
    return pl.pallas_call(
        _postmix_kernel,
        grid=(m // tm,),
        in_specs=[row(d), row(cact.shape[1]), row(attn.shape[1]), row(sg.shape[1]),
                  full(wc), full(wa), full(wo)],
        out_specs=row(d),
        out_shape=jax.ShapeDtypeStruct((m, d), F32),
        compiler_params=_params("parallel"),
        name="postmix",
    )(x, cact, attn, sg, wc, wa, wo)


def _ffn_kernel(x_ref, g_ref, wg_ref, wu_ref, wd_ref, gf_ref, o_ref, h_ref):
    f = pl.program_id(1)

    @pl.when(f == 0)
    def _():
        x = x_ref[...]
        ms = jnp.mean(x * x, axis=-1, keepdims=True)
        h_ref[...] = (x * lax.rsqrt(ms + EPS) * g_ref[...]).astype(h_ref.dtype)
        o_ref[...] = jnp.zeros_like(o_ref)

    h = h_ref[...]
    gate = jnp.dot(h, wg_ref[...], preferred_element_type=F32)
    up = jnp.dot(h, wu_ref[...], preferred_element_type=F32)
    act = (gate * jax.nn.sigmoid(gate) * up).astype(BF16)
    o_ref[...] += jnp.dot(act, wd_ref[...], preferred_element_type=F32)

    @pl.when(f == pl.num_programs(1) - 1)
    def _():
        y = x_ref[...] + o_ref[...]
        ms = jnp.mean(y * y, axis=-1, keepdims=True)
        o_ref[...] = y * lax.rsqrt(ms + EPS) * gf_ref[...]


def _ffn(x, g_ffn, wg, wu, wd, g_final, tm=512, tf=512):
    m, d = x.shape
    dff = wg.shape[1]
    tm = min(tm, m)
    return pl.pallas_call(
        _ffn_kernel,
        grid=(m // tm, dff // tf),
        in_specs=[pl.BlockSpec((tm, d), lambda i, f: (i, 0)),
                  pl.BlockSpec((1, d), lambda i, f: (0, 0)),
                  pl.BlockSpec((d, tf), lambda i, f: (0, f)),
                  pl.BlockSpec((d, tf), lambda i, f: (0, f)),
                  pl.BlockSpec((tf, d), lambda i, f: (f, 0)),
                  pl.BlockSpec((1, d), lambda i, f: (0, 0))],
        out_specs=pl.BlockSpec((tm, d), lambda i, f: (i, 0)),
        out_shape=jax.ShapeDtypeStruct((m, d), F32),
        scratch_shapes=[pltpu.VMEM((tm, d), BF16)],
        compiler_params=_params("parallel", "arbitrary"),
        name="ffn",
    )(x, g_ffn.reshape(1, d), wg, wu, wd, g_final.reshape(1, d))


def _prep_weights(w_in, d_conv, d_attn):
    s = [d_conv, d_conv, d_attn, d_attn, d_attn, IDX_HEADS * IDX_DIM, IDX_DIM, IDX_HEADS]
    pts = [0]
    for n in s:
        pts.append(pts[-1] + n)
    wa, wb, wq, wk, wv, wqi, wki, wwi = [w_in[:, pts[i]:pts[i + 1]] for i in range(8)]
    wgate = w_in[:, pts[8]:]
    d = w_in.shape[0]
    half = min(512, d_conv)
    nt = d_conv // half
    glu = jnp.concatenate([wa.reshape(d, nt, half), wb.reshape(d, nt, half)], axis=2).reshape(d, 2 * d_conv)
    small = jnp.concatenate(
        [wki, wwi, jnp.zeros((d, LANES - IDX_DIM - IDX_HEADS), w_in.dtype)], axis=1)
    cast = lambda w: w.astype(BF16)
    return dict(glu=cast(glu), q=cast(wq), k=cast(wk), v=cast(wv), qi=cast(wqi), small=cast(small),
                gate=cast(wgate))


def _group(x, past_conv, past_k, past_v, past_ki, wts, p, tq, conv_chunk, heads_per_step):
    b, t, d = x.shape
    m = b * t
    x2 = x.reshape(m, d)
    d_conv = p["conv_w"].shape[1]
    d_attn = wts["q"].shape[1]

    h = _rmsnorm(x2, p["g_mix"], BF16)
    half = min(512, d_conv)
    (u,) = _matmul(_glu_kernel, h, wts["glu"], [(half, F32)], tn=2 * half, name="proj_glu")
    (q,) = _matmul(functools.partial(_proj_kernel, scale=HEAD_DIM ** -0.5), h, wts["q"],
                   [(None, BF16)], name="proj_q")
    k32, kbf = _matmul(functools.partial(_proj_kernel, scale=1.0), h, wts["k"],
                       [(None, F32), (None, BF16)], name="proj_k")
    v32, vbf = _matmul(functools.partial(_proj_kernel, scale=1.0), h, wts["v"],
                       [(None, F32), (None, BF16)], name="proj_v")
    (qi,) = _matmul(functools.partial(_proj_kernel, scale=1.0), h, wts["qi"],
                    [(None, BF16)], name="proj_qi")
    ki, wi = _matmul(_idx_small_kernel, h, wts["small"], [(IDX_DIM, F32), (IDX_HEADS, F32)],
                     extra=(p["idx_ln_g"].reshape(1, IDX_DIM), p["idx_ln_b"].reshape(1, IDX_DIM)),
                     name="proj_idx")
    (sg,) = _matmul(_sigmoid_kernel, h, wts["gate"], [(None, BF16)], name="proj_gates")

    u3 = u.reshape(b, t, d_conv)
    ki3 = ki.reshape(b, t, IDX_DIM)
    if past_conv is None:
        cache_pad = jnp.zeros((b, CONV_HALO, d_conv), F32)
        ki_all = ki3
        n_past = 0
    else:
        cache_pad = jnp.pad(past_conv.astype(F32), ((0, 0), (CONV_HALO - (CONV_WIDTH - 1), 0), (0, 0)))
        n_past = past_k.shape[1]
        ki_all = jnp.concatenate([past_ki, ki3], axis=1)
    cact = _conv_branch(u3, cache_pad, p["conv_w"], p["conv_b"], p["conv_ln_g"], p["conv_ln_b"], conv_chunk)

    n_real = n_past + t
    n_keys = -(-n_real // LANES) * LANES
    topk = min(TOPK_MAX, n_real // 4)
    kit = jnp.swapaxes(jnp.pad(ki_all, ((0, 0), (0, n_keys - n_real), (0, 0))), 1, 2).astype(BF16)
    zeros = jnp.zeros_like(kit)
    klo = jnp.concatenate([kit, zeros], axis=1)
    khi = jnp.concatenate([zeros, kit], axis=1)
    bias = _select(qi.reshape(b, t, -1), wi.reshape(b, t, IDX_HEADS), klo, khi, tq, n_past, topk)

    kb3 = kbf.reshape(b, t, d_attn)
    vb3 = vbf.reshape(b, t, d_attn)
    if past_k is None:
        ks, vs = [kb3], [vb3]
    else:
        pad = ((0, 0), (0, n_keys - n_real), (0, 0))
        ks = [past_k.reshape(b, n_past, d_attn), jnp.pad(kb3, pad)]
        vs = [past_v.reshape(b, n_past, d_attn), jnp.pad(vb3, pad)]
    attn = _attention(q.reshape(b, t, d_attn), bias, ks, vs, tq, heads_per_step)

    x1 = _postmix(x2, cact.reshape(m, d_conv), attn.reshape(m, d_attn), sg,
                  p["w_conv_out"], p["w_attn_out"], p["w_out"])
    tail = CONV_WIDTH - 1
    return (x1, u3[:, t - tail:, :], k32.reshape(b, t, N_HEADS, HEAD_DIM),
            v32.reshape(b, t, N_HEADS, HEAD_DIM), ki3)


def kernel(x_prompt, x_sample, cache_conv, cache_k, cache_v, cache_kidx, g_mix, w_in, conv_w, conv_b,
           conv_ln_g, conv_ln_b, w_conv_out, idx_ln_g, idx_ln_b, w_attn_out, w_out, g_ffn, w_gate, w_up,
           w_down, g_final):
    depth = w_in.shape[0]
    assert depth == 1 and x_prompt.shape[1] >= CONV_WIDTH - 1 and x_sample.shape[1] >= CONV_WIDTH - 1
    d_conv = conv_w.shape[2]
    d_attn = w_attn_out.shape[1]
    l = 0
    wts = _prep_weights(w_in[l], d_conv, d_attn)
    p = dict(g_mix=g_mix[l], conv_w=conv_w[l], conv_b=conv_b[l], conv_ln_g=conv_ln_g[l],
             conv_ln_b=conv_ln_b[l], idx_ln_g=idx_ln_g[l], idx_ln_b=idx_ln_b[l],
             w_conv_out=w_conv_out[l].astype(BF16), w_attn_out=w_attn_out[l].astype(BF16),
             w_out=w_out[l].astype(BF16))
    wg, wu, wd = w_gate[l].astype(BF16), w_up[l].astype(BF16), w_down[l].astype(BF16)

    xp, conv_p, k_p, v_p, kidx_p = _group(
        x_prompt, None, None, None, None, wts, p, tq=128, conv_chunk=128, heads_per_step=N_HEADS)
    xs, conv_s, k_s, v_s, kidx_s = _group(
        x_sample, cache_conv[l], cache_k[l], cache_v[l], cache_kidx[l], wts, p,
        tq=x_sample.shape[1], conv_chunk=x_sample.shape[1], heads_per_step=4)

    y_prompt = _ffn(xp, g_ffn[l], wg, wu, wd, g_final).reshape(x_prompt.shape)
    y_sample = _ffn(xs, g_ffn[l], wg, wu, wd, g_final).reshape(x_sample.shape)
    return (y_prompt, y_sample, conv_p[None], k_p[None], v_p[None], kidx_p[None],
            conv_s[None], k_s[None], v_s[None], kidx_s[None])
```

```python
import functools

import jax
import jax.numpy as jnp
from jax import lax
from jax.experimental import pallas as pl
from jax.experimental.pallas import tpu as pltpu

F32 = jnp.float32
BF16 = jnp.bfloat16

CHUNK = 64
CONV_WIDTH = 31
N_HEADS = 16
HEAD_DIM = 128
IDX_HEADS = 16
IDX_DIM = 64
TOPK_MAX = 256
EPS = 1e-6

LANES = 128
SUBLANES = 8
CONV_HALO = 32
CONV_ROWS = 32
SEARCH_UNROLL = 4
VMEM_LIMIT = 56 * 2**20


def _params(*sem):
    return pltpu.CompilerParams(dimension_semantics=sem, vmem_limit_bytes=VMEM_LIMIT)


def _rms_kernel(x_ref, g_ref, o_ref):
    x = x_ref[...]
    ms = jnp.mean(x * x, axis=-1, keepdims=True)
    o_ref[...] = (x * lax.rsqrt(ms + EPS) * g_ref[...]).astype(o_ref.dtype)


def _rmsnorm(x, g, out_dtype, tm=512):
    m, d = x.shape
    tm = min(tm, m)
    return pl.pallas_call(
        _rms_kernel,
        grid=(m // tm,),
        in_specs=[pl.BlockSpec((tm, d), lambda i: (i, 0)),
                  pl.BlockSpec((1, d), lambda i: (0, 0))],
        out_specs=pl.BlockSpec((tm, d), lambda i: (i, 0)),
        out_shape=jax.ShapeDtypeStruct((m, d), out_dtype),
        compiler_params=_params("parallel"),
        name="rmsnorm",
    )(x, g.reshape(1, d))


def _proj_kernel(h_ref, w_ref, *o_refs, scale):
    z = jnp.dot(h_ref[...], w_ref[...], preferred_element_type=F32)
    if scale != 1.0:
        z = z * scale
    for o_ref in o_refs:
        o_ref[...] = z.astype(o_ref.dtype)


def _glu_kernel(h_ref, w_ref, o_ref):
    z = jnp.dot(h_ref[...], w_ref[...], preferred_element_type=F32)
    half = z.shape[1] // 2
    o_ref[...] = z[:, :half] * jax.nn.sigmoid(z[:, half:])


def _sigmoid_kernel(h_ref, w_ref, o_ref):
    z = jnp.dot(h_ref[...], w_ref[...], preferred_element_type=F32)
    o_ref[...] = jax.nn.sigmoid(z).astype(o_ref.dtype)


def _idx_small_kernel(h_ref, w_ref, g_ref, b_ref, ki_ref, wi_ref):
    z = jnp.dot(h_ref[...], w_ref[...], preferred_element_type=F32)
    kz = z[:, :IDX_DIM]
    mu = jnp.mean(kz, axis=-1, keepdims=True)
    var = jnp.mean(jnp.square(kz - mu), axis=-1, keepdims=True)
    ki_ref[...] = (kz - mu) * lax.rsqrt(var + EPS) * g_ref[...] + b_ref[...]
    wi_ref[...] = z[:, IDX_DIM:IDX_DIM + IDX_HEADS]


def _matmul(body, h, w, outs, extra=(), tm=1024, tn=1024, name="proj"):
    m, k = h.shape
    n = w.shape[1]
    tm = min(tm, m)
    tn = min(tn, n)
    nt = n // tn
    outs = [(tn if c is None else c, dt) for c, dt in outs]
    in_specs = [pl.BlockSpec((tm, k), lambda i, j: (i, 0)),
                pl.BlockSpec((k, tn), lambda i, j: (0, j))]
    in_specs += [pl.BlockSpec(e.shape, lambda i, j: (0, 0)) for e in extra]
    res = pl.pallas_call(
        body,
        grid=(m // tm, nt),
        in_specs=in_specs,
        out_specs=[pl.BlockSpec((tm, c), lambda i, j: (i, j)) for c, _ in outs],
        out_shape=[jax.ShapeDtypeStruct((m, c * nt), dt) for c, dt in outs],
        compiler_params=_params("parallel", "arbitrary"),
        name=name,
    )(h, w, *extra)
    return res


def _conv_kernel(cur_ref, prev_ref, cache_ref, cw_ref, cb_ref, g_ref, b_ref, o_ref, ext_ref, *, tc):
    c = pl.program_id(1)
    rows = CONV_HALO + tc
    ext_ref[0, 0:CONV_HALO, :] = jnp.where(c == 0, cache_ref[0], prev_ref[0])
    ext_ref[0, CONV_HALO:rows, :] = cur_ref[0]
    for r in range(1, SUBLANES):
        ext_ref[r, 0:rows - SUBLANES, :] = ext_ref[0, r:r + rows - SUBLANES, :]
    shift = CONV_HALO - (CONV_WIDTH - 1)

    for base in range(0, tc, CONV_ROWS):
        acc = jnp.broadcast_to(cb_ref[...], (CONV_ROWS, cb_ref.shape[1]))
        for w in range(CONV_WIDTH):
            r = (shift + w) % SUBLANES
            start = base + shift + w - r
            acc = acc + ext_ref[r, start:start + CONV_ROWS, :] * cw_ref[w:w + 1, :]
        mu = jnp.mean(acc, axis=-1, keepdims=True)
        var = jnp.mean(jnp.square(acc - mu), axis=-1, keepdims=True)
        y = (acc - mu) * lax.rsqrt(var + EPS) * g_ref[...] + b_ref[...]
        o_ref[0, base:base + CONV_ROWS, :] = (y * jax.nn.sigmoid(y)).astype(o_ref.dtype)


def _conv_branch(u, cache_pad, conv_w, conv_b, ln_g, ln_b, tc):
    b, t, c = u.shape
    tc = min(tc, t)
    per = tc // CONV_HALO
    return pl.pallas_call(
        functools.partial(_conv_kernel, tc=tc),
        grid=(b, t // tc),
        in_specs=[pl.BlockSpec((1, tc, c), lambda i, j: (i, j, 0)),
                  pl.BlockSpec((1, CONV_HALO, c), lambda i, j: (i, jnp.maximum(j * per - 1, 0), 0)),
                  pl.BlockSpec((1, CONV_HALO, c), lambda i, j: (i, 0, 0)),
                  pl.BlockSpec((CONV_WIDTH, c), lambda i, j: (0, 0)),
                  pl.BlockSpec((1, c), lambda i, j: (0, 0)),
                  pl.BlockSpec((1, c), lambda i, j: (0, 0)),
                  pl.BlockSpec((1, c), lambda i, j: (0, 0))],
        out_specs=pl.BlockSpec((1, tc, c), lambda i, j: (i, j, 0)),
        out_shape=jax.ShapeDtypeStruct((b, t, c), BF16),
        scratch_shapes=[pltpu.VMEM((SUBLANES, CONV_HALO + tc, c), F32)],
        compiler_params=_params("parallel", "arbitrary"),
        name="conv_branch",
    )(u, u, cache_pad, conv_w, conv_b.reshape(1, c), ln_g.reshape(1, c), ln_b.reshape(1, c))


def _select_kernel(qi_ref, wi_ref, klo_ref, khi_ref, bias_ref, score_ref, *, tq, n_keys, qpos0, topk):
    j = pl.program_id(1)
    wi = wi_ref[0] * (IDX_HEADS ** -0.5 * IDX_DIM ** -0.5)
    acc = None
    for p in range(IDX_HEADS // 2):
        qt = qi_ref[0, :, p * LANES:(p + 1) * LANES]
        for head, k_ref in ((2 * p, klo_ref), (2 * p + 1, khi_ref)):
            logits = jnp.dot(qt, k_ref[0], preferred_element_type=F32)
            term = wi[:, head:head + 1] * jnp.maximum(logits, 0.0)
            acc = term if acc is None else acc + term

    kpos = lax.broadcasted_iota(jnp.int32, (tq, n_keys), 1)
    qpos = qpos0 + j * tq + lax.broadcasted_iota(jnp.int32, (tq, 1), 0)
    limit = (qpos // CHUNK + 1) * CHUNK
    adm = kpos < limit
    score_ref[...] = jnp.where(adm, acc, -jnp.inf)
    kf = float(topk)
    n_adm = jnp.minimum(limit, n_keys).astype(F32)
    searching = (n_adm > kf).astype(F32)
    lo0 = jnp.min(jnp.where(adm, acc, jnp.inf), axis=1, keepdims=True)
    hi0 = jnp.max(score_ref[...], axis=1, keepdims=True)
    thr0 = jnp.full((tq, 1), -jnp.inf, F32)
    tie0 = jnp.zeros((tq, 1), F32)

    def count_ge(t):
        return jnp.sum(jnp.where(score_ref[...] >= t, 1.0, 0.0), axis=1, keepdims=True)

    def step(state):
        lo, hi, thr, active, tie = state
        mid = 0.5 * lo + 0.5 * hi
        cnt = count_ge(mid)
        on = active > 0.0
        ge = cnt >= kf
        exact = cnt == kf
        collapsed = jnp.logical_or(mid <= lo, mid >= hi)
        lo = jnp.where(jnp.logical_and(on, ge), mid, lo)
        hi = jnp.where(jnp.logical_and(on, jnp.logical_not(ge)), mid, hi)
        thr = jnp.where(jnp.logical_and(on, exact), mid, thr)
        tie = jnp.where(jnp.logical_and(on, jnp.logical_and(collapsed, jnp.logical_not(exact))), 1.0, tie)
        active = jnp.where(jnp.logical_or(exact, collapsed), 0.0, active)
        return lo, hi, thr, active, tie

    def body(carry):
        state = carry[:5]
        for _ in range(SEARCH_UNROLL):
            state = step(state)
        return state + (jnp.max(state[3]),)

    lo, hi, thr, _, tie, _ = lax.while_loop(
        lambda carry: carry[5] > 0.0, body,
        (lo0, hi0, thr0, searching, tie0, jnp.max(searching)))

    score = score_ref[...]
    bias_ref[0] = jnp.where(jnp.logical_and(score >= thr, adm), 0.0, -jnp.inf).astype(bias_ref.dtype)

    @pl.when(jnp.max(tie) > 0.0)
    def _():
        score = score_ref[...]
        t = jnp.where(count_ge(hi) >= kf, hi, lo)
        gt = score > t
        eq = score == t
        need = kf - jnp.sum(jnp.where(gt, 1.0, 0.0), axis=1, keepdims=True)
        r = lax.broadcasted_iota(jnp.int32, (LANES, LANES), 0)
        cidx = lax.broadcasted_iota(jnp.int32, (LANES, LANES), 1)
        tri = jnp.where(r <= cidx, 1.0, 0.0).astype(BF16)
        carry = jnp.zeros((tq, 1), F32)
        for s in range(n_keys // LANES):
            sl = slice(s * LANES, (s + 1) * LANES)
            e = jnp.where(eq[:, sl], 1.0, 0.0)
            pre = jnp.dot(e.astype(BF16), tri, preferred_element_type=F32) + carry
            carry = pre[:, LANES - 1:LANES]
            keep = jnp.logical_or(gt[:, sl], jnp.logical_and(eq[:, sl], pre <= need))
            plain = jnp.logical_and(score[:, sl] >= thr, adm[:, sl])
            bias = jnp.where(tie > 0.0, jnp.where(keep, 0.0, -jnp.inf), jnp.where(plain, 0.0, -jnp.inf))
            bias_ref[0, :, sl] = bias.astype(bias_ref.dtype)


def _select(qi, wi, klo, khi, tq, q_start, n_q, n_keys, pos0, topk):
    b, _, dq = qi.shape
    off = q_start // tq
    return pl.pallas_call(
        functools.partial(_select_kernel, tq=tq, n_keys=n_keys, qpos0=pos0 + q_start, topk=topk),
        grid=(b, n_q // tq),
        in_specs=[pl.BlockSpec((1, tq, dq), lambda i, j: (i, off + j, 0)),
                  pl.BlockSpec((1, tq, IDX_HEADS), lambda i, j: (i, off + j, 0)),
                  pl.BlockSpec((1, LANES, n_keys), lambda i, j: (i, 0, 0)),
                  pl.BlockSpec((1, LANES, n_keys), lambda i, j: (i, 0, 0))],
        out_specs=pl.BlockSpec((1, tq, n_keys), lambda i, j: (i, j, 0)),
        out_shape=jax.ShapeDtypeStruct((b, n_q, n_keys), BF16),
        scratch_shapes=[pltpu.VMEM((tq, n_keys), F32)],
        compiler_params=_params("parallel", "arbitrary"),
        name="select_topk",
    )(qi, wi, klo, khi)


def _attn_kernel(q_ref, bias_ref, *refs, heads, seg_lens):
    nseg = len(seg_lens)
    k_refs, v_refs, o_ref = refs[:nseg], refs[nseg:2 * nseg], refs[-1]
    offs = [sum(seg_lens[:i]) for i in range(nseg)]
    biases = [bias_ref[0, :, o:o + n].astype(F32) for o, n in zip(offs, seg_lens)]
    for h in range(heads):
        sl = slice(h * HEAD_DIM, (h + 1) * HEAD_DIM)
        qh = q_ref[0, :, sl]
        scores = []
        for k_ref, bias in zip(k_refs, biases):
            scores.append(lax.dot_general(qh, k_ref[0, :, sl], (((1,), (1,)), ((), ())),
                                          preferred_element_type=F32) + bias)
        m = scores[0].max(axis=-1, keepdims=True)
        for s in scores[1:]:
            m = jnp.maximum(m, s.max(axis=-1, keepdims=True))
        denom = None
        out = None
        for s, v_ref in zip(scores, v_refs):
            p = jnp.exp(s - m)
            psum = p.sum(axis=-1, keepdims=True)
            pv = jnp.dot(p.astype(BF16), v_ref[0, :, sl], preferred_element_type=F32)
            denom = psum if denom is None else denom + psum
            out = pv if out is None else out + pv
        o_ref[0, :, sl] = (out / denom).astype(o_ref.dtype)


def _attention(q, bias, segs, q_start, heads_per_step, out_prev=None):
    b, t, d = q.shape
    n_q = bias.shape[1]
    qb = q_start // n_q
    dh = heads_per_step * HEAD_DIM
    seg_lens = tuple(n for _, _, n in segs)
    kv_specs = [pl.BlockSpec((1, n, dh), lambda i, g: (i, 0, g)) for n in seg_lens]
    in_specs = [pl.BlockSpec((1, n_q, dh), lambda i, g: (i, qb, g)),
                pl.BlockSpec((1, n_q, sum(seg_lens)), lambda i, g: (i, 0, 0))] + kv_specs + kv_specs
    args = [q, bias] + [k for k, _, _ in segs] + [v for _, v, _ in segs]
    aliases = {}
    if out_prev is not None:
        in_specs.append(pl.BlockSpec(memory_space=pl.ANY))
        aliases = {len(args): 0}
        args.append(out_prev)
    return pl.pallas_call(
        functools.partial(_attn_kernel, heads=heads_per_step, seg_lens=seg_lens),
        grid=(b, d // dh),
        in_specs=in_specs,
        out_specs=pl.BlockSpec((1, n_q, dh), lambda i, g: (i, qb, g)),
        out_shape=jax.ShapeDtypeStruct((b, t, d), BF16),
        input_output_aliases=aliases,
        compiler_params=_params("parallel", "arbitrary"),
        name="sparse_attention",
    )(*args)


def _postmix_kernel(x_ref, c_ref, a_ref, sg_ref, wc_ref, wa_ref, wo_ref, o_ref):
    d = x_ref.shape[1]
    c = jnp.dot(c_ref[...], wc_ref[...], preferred_element_type=F32)
    a = jnp.dot(a_ref[...], wa_ref[...], preferred_element_type=F32)
    m = sg_ref[:, :d].astype(F32) * c + sg_ref[:, d:].astype(F32) * a
    o_ref[...] = x_ref[...] + jnp.dot(m.astype(BF16), wo_ref[...], preferred_element_type=F32)


def _postmix(x, cact, attn, sg, wc, wa, wo, tm=256):
    m, d = x.shape
    tm = min(tm, m)
    row = lambda width: pl.BlockSpec((tm, width), lambda i: (i, 0))
    full = lambda w: pl.BlockSpec(w.shape, lambda i: (0, 0), pipeline_mode=pl.Buffered(1))
    return pl.pallas_call(
        _postmix_kernel,
        grid=(m // tm,),
        in_specs=[row(d), row(cact.shape[1]), row(attn.shape[1]), row(sg.shape[1]),
                  full(wc), full(wa), full(wo)],
        out_specs=row(d),
        out_shape=jax.ShapeDtypeStruct((m, d), F32),
        compiler_params=_params("parallel"),
        name="postmix",
    )(x, cact, attn, sg, wc, wa, wo)


def _ffn_kernel(x_ref, g_ref, wg_ref, wu_ref, wd_ref, gf_ref, o_ref, h_ref):
    f = pl.program_id(1)

    @pl.when(f == 0)
    def _():
        x = x_ref[...]
        ms = jnp.mean(x * x, axis=-1, keepdims=True)
        h_ref[...] = (x * lax.rsqrt(ms + EPS) * g_ref[...]).astype(h_ref.dtype)
        o_ref[...] = jnp.zeros_like(o_ref)

    h = h_ref[...]
    gate = jnp.dot(h, wg_ref[...], preferred_element_type=F32)
    up = jnp.dot(h, wu_ref[...], preferred_element_type=F32)
    act = (gate * jax.nn.sigmoid(gate) * up).astype(BF16)
    o_ref[...] += jnp.dot(act, wd_ref[...], preferred_element_type=F32)

    @pl.when(f == pl.num_programs(1) - 1)
    def _():
        y = x_ref[...] + o_ref[...]
        ms = jnp.mean(y * y, axis=-1, keepdims=True)
        o_ref[...] = y * lax.rsqrt(ms + EPS) * gf_ref[...]


def _ffn(x, g_ffn, wg, wu, wd, g_final, tm=512, tf=512):
    m, d = x.shape
    dff = wg.shape[1]
    tm = min(tm, m)
    return pl.pallas_call(
        _ffn_kernel,
        grid=(m // tm, dff // tf),
        in_specs=[pl.BlockSpec((tm, d), lambda i, f: (i, 0)),
                  pl.BlockSpec((1, d), lambda i, f: (0, 0)),
                  pl.BlockSpec((d, tf), lambda i, f: (0, f)),
                  pl.BlockSpec((d, tf), lambda i, f: (0, f)),
                  pl.BlockSpec((tf, d), lambda i, f: (f, 0)),
                  pl.BlockSpec((1, d), lambda i, f: (0, 0))],
        out_specs=pl.BlockSpec((tm, d), lambda i, f: (i, 0)),
        out_shape=jax.ShapeDtypeStruct((m, d), F32),
        scratch_shapes=[pltpu.VMEM((tm, d), BF16)],
        compiler_params=_params("parallel", "arbitrary"),
        name="ffn",
    )(x, g_ffn.reshape(1, d), wg, wu, wd, g_final.reshape(1, d))


def _prep_weights(w_in, d_conv, d_attn):
    s = [d_conv, d_conv, d_attn, d_attn, d_attn, IDX_HEADS * IDX_DIM, IDX_DIM, IDX_HEADS]
    pts = [0]
    for n in s:
        pts.append(pts[-1] + n)
    wa, wb, wq, wk, wv, wqi, wki, wwi = [w_in[:, pts[i]:pts[i + 1]] for i in range(8)]
    wgate = w_in[:, pts[8]:]
    d = w_in.shape[0]
    half = min(512, d_conv)
    nt = d_conv // half
    glu = jnp.concatenate([wa.reshape(d, nt, half), wb.reshape(d, nt, half)], axis=2).reshape(d, 2 * d_conv)
    small = jnp.concatenate(
        [wki, wwi, jnp.zeros((d, LANES - IDX_DIM - IDX_HEADS), w_in.dtype)], axis=1)
    cast = lambda w: w.astype(BF16)
    return dict(glu=cast(glu), q=cast(wq), k=cast(wk), v=cast(wv), qi=cast(wqi), small=cast(small),
                gate=cast(wgate))


def _group(x, past_conv, past_k, past_v, past_ki, layer, wts, p, tq, q_class, conv_chunk, heads_per_step):
    b, t, d = x.shape
    q_class = min(q_class, t)
    m = b * t
    x2 = x.reshape(m, d)
    d_conv = p["conv_w"].shape[1]
    d_attn = wts["q"].shape[1]

    h = _rmsnorm(x2, p["g_mix"], BF16)
    half = min(512, d_conv)
    (u,) = _matmul(_glu_kernel, h, wts["glu"], [(half, F32)], tn=2 * half, name="proj_glu")
    (q,) = _matmul(functools.partial(_proj_kernel, scale=HEAD_DIM ** -0.5), h, wts["q"],
                   [(None, BF16)], name="proj_q")
    k32, kbf = _matmul(functools.partial(_proj_kernel, scale=1.0), h, wts["k"],
                       [(None, F32), (None, BF16)], name="proj_k")
    v32, vbf = _matmul(functools.partial(_proj_kernel, scale=1.0), h, wts["v"],
                       [(None, F32), (None, BF16)], name="proj_v")
    (qi,) = _matmul(functools.partial(_proj_kernel, scale=1.0), h, wts["qi"],
                    [(None, BF16)], name="proj_qi")
    ki, wi = _matmul(_idx_small_kernel, h, wts["small"], [(IDX_DIM, F32), (IDX_HEADS, F32)],
                     extra=(p["idx_ln_g"].reshape(1, IDX_DIM), p["idx_ln_b"].reshape(1, IDX_DIM)),
                     name="proj_idx")
    (sg,) = _matmul(_sigmoid_kernel, h, wts["gate"], [(None, BF16)], name="proj_gates")

    u3 = u.reshape(b, t, d_conv)
    ki3 = ki.reshape(b, t, IDX_DIM)
    if past_conv is None:
        cache_pad = jnp.zeros((b, CONV_HALO, d_conv), F32)
        ki_all = ki3
        n_past = 0
    else:
        cache_pad = jnp.pad(past_conv.astype(F32), ((0, 0), (CONV_HALO - (CONV_WIDTH - 1), 0), (0, 0)))
        n_past = past_k.shape[2]
        ki_all = jnp.concatenate([past_ki, ki3], axis=1)
    cact = _conv_branch(u3, cache_pad, p["conv_w"], p["conv_b"], p["conv_ln_g"], p["conv_ln_b"], conv_chunk)

    n_real = n_past + t
    n_keys = -(-n_real // LANES) * LANES
    topk = min(TOPK_MAX, n_real // 4)
    kit = jnp.swapaxes(jnp.pad(ki_all, ((0, 0), (0, n_keys - n_real), (0, 0))), 1, 2).astype(BF16)
    zeros = jnp.zeros_like(kit)
    klo = jnp.concatenate([kit, zeros], axis=1)
    khi = jnp.concatenate([zeros, kit], axis=1)
    qi3 = qi.reshape(b, t, -1)
    wi3 = wi.reshape(b, t, IDX_HEADS)
    q3 = q.reshape(b, t, d_attn)
    kb3 = kbf.reshape(b, t, d_attn)
    vb3 = vbf.reshape(b, t, d_attn)
    if past_k is None:
        attn = None
        for q_start in range(0, t, q_class):
            n_k = q_start + q_class
            bias = _select(qi3, wi3, klo, khi, tq, q_start, q_class, n_k, 0, topk)
            attn = _attention(q3, bias, [(kb3, vb3, n_k)], q_start, heads_per_step, out_prev=attn)
    else:
        bias = _select(qi3, wi3, klo, khi, tq, 0, t, n_keys, n_past, topk)
        pad = ((0, 0), (0, n_keys - n_real), (0, 0))
        flat = lambda c: c[layer].reshape(b, n_past, d_attn).astype(BF16)
        segs = [(flat(past_k), flat(past_v), n_past),
                (jnp.pad(kb3, pad), jnp.pad(vb3, pad), n_keys - n_past)]
        attn = _attention(q3, bias, segs, 0, heads_per_step)

    x1 = _postmix(x2, cact.reshape(m, d_conv), attn.reshape(m, d_attn), sg,
                  p["w_conv_out"], p["w_attn_out"], p["w_out"])
    tail = CONV_WIDTH - 1
    return (x1, u3[:, t - tail:, :], k32.reshape(b, t, N_HEADS, HEAD_DIM),
            v32.reshape(b, t, N_HEADS, HEAD_DIM), ki3)


def kernel(x_prompt, x_sample, cache_conv, cache_k, cache_v, cache_kidx, g_mix, w_in, conv_w, conv_b,
           conv_ln_g, conv_ln_b, w_conv_out, idx_ln_g, idx_ln_b, w_attn_out, w_out, g_ffn, w_gate, w_up,
           w_down, g_final):
    depth = w_in.shape[0]
    assert depth == 1 and x_prompt.shape[1] >= CONV_WIDTH - 1 and x_sample.shape[1] >= CONV_WIDTH - 1
    d_conv = conv_w.shape[2]
    d_attn = w_attn_out.shape[1]
    l = 0
    wts = _prep_weights(w_in[l], d_conv, d_attn)
    p = dict(g_mix=g_mix[l], conv_w=conv_w[l], conv_b=conv_b[l], conv_ln_g=conv_ln_g[l],
             conv_ln_b=conv_ln_b[l], idx_ln_g=idx_ln_g[l], idx_ln_b=idx_ln_b[l],
             w_conv_out=w_conv_out[l].astype(BF16), w_attn_out=w_attn_out[l].astype(BF16),
             w_out=w_out[l].astype(BF16))
    wg, wu, wd = w_gate[l].astype(BF16), w_up[l].astype(BF16), w_down[l].astype(BF16)

    xp, conv_p, k_p, v_p, kidx_p = _group(
        x_prompt, None, None, None, None, l, wts, p, tq=128, q_class=512, conv_chunk=128,
        heads_per_step=8)
    xs, conv_s, k_s, v_s, kidx_s = _group(
        x_sample, cache_conv[l], cache_k, cache_v, cache_kidx[l], l, wts, p,
        tq=x_sample.shape[1], q_class=x_sample.shape[1], conv_chunk=x_sample.shape[1],
        heads_per_step=8)

    y_prompt = _ffn(xp, g_ffn[l], wg, wu, wd, g_final).reshape(x_prompt.shape)
    y_sample = _ffn(xs, g_ffn[l], wg, wu, wd, g_final).reshape(x_sample.shape)
    return (y_prompt, y_sample, conv_p[None], k_p[None], v_p[None], kidx_p[None],
            conv_s[None], k_s[None], v_s[None], kidx_s[None])
```

```python
import functools

import jax
import jax.numpy as jnp
from jax import lax
from jax.experimental import pallas as pl
from jax.experimental.pallas import tpu as pltpu

F32 = jnp.float32
BF16 = jnp.bfloat16

CHUNK = 64
CONV_WIDTH = 31
N_HEADS = 16
HEAD_DIM = 128
IDX_HEADS = 16
IDX_DIM = 64
TOPK_MAX = 256
EPS = 1e-6

LANES = 128
SUBLANES = 8
CONV_HALO = 32
CONV_ROWS = 32
SEARCH_UNROLL = 4
VMEM_LIMIT = 56 * 2**20


def _params(*sem):
    return pltpu.CompilerParams(dimension_semantics=sem, vmem_limit_bytes=VMEM_LIMIT)


def _rms_kernel(x_ref, g_ref, o_ref):
    x = x_ref[...]
    ms = jnp.mean(x * x, axis=-1, keepdims=True)
    o_ref[...] = (x * lax.rsqrt(ms + EPS) * g_ref[...]).astype(o_ref.dtype)


def _rmsnorm(x, g, out_dtype, tm=512):
    m, d = x.shape
    tm = min(tm, m)
    return pl.pallas_call(
        _rms_kernel,
        grid=(m // tm,),
        in_specs=[pl.BlockSpec((tm, d), lambda i: (i, 0)),
                  pl.BlockSpec((1, d), lambda i: (0, 0))],
        out_specs=pl.BlockSpec((tm, d), lambda i: (i, 0)),
        out_shape=jax.ShapeDtypeStruct((m, d), out_dtype),
        compiler_params=_params("parallel"),
        name="rmsnorm",
    )(x, g.reshape(1, d))


def _proj_kernel(h_ref, w_ref, *o_refs, scale):
    z = jnp.dot(h_ref[...], w_ref[...], preferred_element_type=F32)
    if scale != 1.0:
        z = z * scale
    for o_ref in o_refs:
        o_ref[...] = z.astype(o_ref.dtype)


def _glu_kernel(h_ref, w_ref, o_ref):
    z = jnp.dot(h_ref[...], w_ref[...], preferred_element_type=F32)
    half = z.shape[1] // 2
    o_ref[...] = z[:, :half] * jax.nn.sigmoid(z[:, half:])


def _sigmoid_kernel(h_ref, w_ref, o_ref):
    z = jnp.dot(h_ref[...], w_ref[...], preferred_element_type=F32)
    o_ref[...] = jax.nn.sigmoid(z).astype(o_ref.dtype)


def _idx_small_kernel(h_ref, w_ref, g_ref, b_ref, ki_ref, wi_ref):
    z = jnp.dot(h_ref[...], w_ref[...], preferred_element_type=F32)
    kz = z[:, :IDX_DIM]
    mu = jnp.mean(kz, axis=-1, keepdims=True)
    var = jnp.mean(jnp.square(kz - mu), axis=-1, keepdims=True)
    ki_ref[...] = (kz - mu) * lax.rsqrt(var + EPS) * g_ref[...] + b_ref[...]
    wi_ref[...] = z[:, IDX_DIM:IDX_DIM + IDX_HEADS]


def _matmul(body, h, w, outs, extra=(), tm=1024, tn=1024, name="proj"):
    m, k = h.shape
    n = w.shape[1]
    tm = min(tm, m)
    tn = min(tn, n)
    nt = n // tn
    outs = [(tn if c is None else c, dt) for c, dt in outs]
    in_specs = [pl.BlockSpec((tm, k), lambda i, j: (i, 0)),
                pl.BlockSpec((k, tn), lambda i, j: (0, j))]
    in_specs += [pl.BlockSpec(e.shape, lambda i, j: (0, 0)) for e in extra]
    res = pl.pallas_call(
        body,
        grid=(m // tm, nt),
        in_specs=in_specs,
        out_specs=[pl.BlockSpec((tm, c), lambda i, j: (i, j)) for c, _ in outs],
        out_shape=[jax.ShapeDtypeStruct((m, c * nt), dt) for c, dt in outs],
        compiler_params=_params("parallel", "arbitrary"),
        name=name,
    )(h, w, *extra)
    return res


def _conv_kernel(cur_ref, prev_ref, cache_ref, cw_ref, cb_ref, g_ref, b_ref, o_ref, ext_ref, *, tc):
    c = pl.program_id(1)
    rows = CONV_HALO + tc
    ext_ref[0, 0:CONV_HALO, :] = jnp.where(c == 0, cache_ref[0], prev_ref[0])
    ext_ref[0, CONV_HALO:rows, :] = cur_ref[0]
    for r in range(1, SUBLANES):
        ext_ref[r, 0:rows - SUBLANES, :] = ext_ref[0, r:r + rows - SUBLANES, :]
    shift = CONV_HALO - (CONV_WIDTH - 1)

    for base in range(0, tc, CONV_ROWS):
        acc = jnp.broadcast_to(cb_ref[...], (CONV_ROWS, cb_ref.shape[1]))
        for w in range(CONV_WIDTH):
            r = (shift + w) % SUBLANES
            start = base + shift + w - r
            acc = acc + ext_ref[r, start:start + CONV_ROWS, :] * cw_ref[w:w + 1, :]
        mu = jnp.mean(acc, axis=-1, keepdims=True)
        var = jnp.mean(jnp.square(acc - mu), axis=-1, keepdims=True)
        y = (acc - mu) * lax.rsqrt(var + EPS) * g_ref[...] + b_ref[...]
        o_ref[0, base:base + CONV_ROWS, :] = (y * jax.nn.sigmoid(y)).astype(o_ref.dtype)


def _conv_branch(u, cache_pad, conv_w, conv_b, ln_g, ln_b, tc):
    b, t, c = u.shape
    tc = min(tc, t)
    per = tc // CONV_HALO
    return pl.pallas_call(
        functools.partial(_conv_kernel, tc=tc),
        grid=(b, t // tc),
        in_specs=[pl.BlockSpec((1, tc, c), lambda i, j: (i, j, 0)),
                  pl.BlockSpec((1, CONV_HALO, c), lambda i, j: (i, jnp.maximum(j * per - 1, 0), 0)),
                  pl.BlockSpec((1, CONV_HALO, c), lambda i, j: (i, 0, 0)),
                  pl.BlockSpec((CONV_WIDTH, c), lambda i, j: (0, 0)),
                  pl.BlockSpec((1, c), lambda i, j: (0, 0)),
                  pl.BlockSpec((1, c), lambda i, j: (0, 0)),
                  pl.BlockSpec((1, c), lambda i, j: (0, 0))],
        out_specs=pl.BlockSpec((1, tc, c), lambda i, j: (i, j, 0)),
        out_shape=jax.ShapeDtypeStruct((b, t, c), BF16),
        scratch_shapes=[pltpu.VMEM((SUBLANES, CONV_HALO + tc, c), F32)],
        compiler_params=_params("parallel", "arbitrary"),
        name="conv_branch",
    )(u, u, cache_pad, conv_w, conv_b.reshape(1, c), ln_g.reshape(1, c), ln_b.reshape(1, c))


def _select_kernel(qi_ref, wi_ref, klo_ref, khi_ref, bias_ref, score_ref, *, tq, n_keys, qpos0, topk):
    j = pl.program_id(1)
    wi = wi_ref[0] * (IDX_HEADS ** -0.5 * IDX_DIM ** -0.5)
    acc = None
    for p in range(IDX_HEADS // 2):
        qt = qi_ref[0, :, p * LANES:(p + 1) * LANES]
        for head, k_ref in ((2 * p, klo_ref), (2 * p + 1, khi_ref)):
            logits = jnp.dot(qt, k_ref[0], preferred_element_type=F32)
            term = wi[:, head:head + 1] * jnp.maximum(logits, 0.0)
            acc = term if acc is None else acc + term

    kpos = lax.broadcasted_iota(jnp.int32, (tq, n_keys), 1)
    qpos = qpos0 + j * tq + lax.broadcasted_iota(jnp.int32, (tq, 1), 0)
    limit = (qpos // CHUNK + 1) * CHUNK
    adm = kpos < limit
    score_ref[...] = jnp.where(adm, acc, -jnp.inf)
    kf = float(topk)
    searching = jnp.minimum(limit, n_keys) > topk
    lo0 = jnp.min(jnp.where(adm, acc, jnp.inf), axis=1, keepdims=True)
    hi0 = jnp.max(score_ref[...], axis=1, keepdims=True)
    lo0 = jnp.where(searching, lo0, -jnp.inf)
    hi0 = jnp.where(searching, hi0, -jnp.inf)

    def count_ge(t):
        return jnp.sum(jnp.where(score_ref[...] >= t, 1.0, 0.0), axis=1, keepdims=True)

    def step(lo, hi):
        mid = 0.5 * lo + 0.5 * hi
        cnt = count_ge(mid)
        return jnp.where(cnt >= kf, mid, lo), jnp.where(cnt > kf, hi, mid)

    def unresolved(lo, hi):
        mid = 0.5 * lo + 0.5 * hi
        inside = jnp.logical_and(mid > lo, mid < hi)
        return jnp.max(jnp.where(inside, 1.0, 0.0))

    def body(carry):
        lo, hi, _ = carry
        for _ in range(SEARCH_UNROLL):
            lo, hi = step(lo, hi)
        return lo, hi, unresolved(lo, hi)

    lo, hi, _ = lax.while_loop(lambda carry: carry[2] > 0.0, body, (lo0, hi0, unresolved(lo0, hi0)))

    score = score_ref[...]
    bias_ref[0] = jnp.where(jnp.logical_and(score >= lo, adm), 0.0, -jnp.inf).astype(bias_ref.dtype)

    tie = jnp.where(jnp.logical_and(searching, count_ge(lo) != kf), 1.0, 0.0)

    @pl.when(jnp.max(tie) > 0.0)
    def _():
        score = score_ref[...]
        t = jnp.where(count_ge(hi) >= kf, hi, lo)
        gt = score > t
        eq = score == t
        need = kf - jnp.sum(jnp.where(gt, 1.0, 0.0), axis=1, keepdims=True)
        r = lax.broadcasted_iota(jnp.int32, (LANES, LANES), 0)
        cidx = lax.broadcasted_iota(jnp.int32, (LANES, LANES), 1)
        tri = jnp.where(r <= cidx, 1.0, 0.0).astype(BF16)
        carry = jnp.zeros((tq, 1), F32)
        for s in range(n_keys // LANES):
            sl = slice(s * LANES, (s + 1) * LANES)
            e = jnp.where(eq[:, sl], 1.0, 0.0)
            pre = jnp.dot(e.astype(BF16), tri, preferred_element_type=F32) + carry
            carry = pre[:, LANES - 1:LANES]
            keep = jnp.logical_or(gt[:, sl], jnp.logical_and(eq[:, sl], pre <= need))
            plain = jnp.logical_and(score[:, sl] >= lo, adm[:, sl])
            bias = jnp.where(tie > 0.0, jnp.where(keep, 0.0, -jnp.inf), jnp.where(plain, 0.0, -jnp.inf))
            bias_ref[0, :, sl] = bias.astype(bias_ref.dtype)


def _select(qi, wi, klo, khi, tq, q_start, n_q, n_keys, pos0, topk):
    b, _, dq = qi.shape
    off = q_start // tq
    return pl.pallas_call(
        functools.partial(_select_kernel, tq=tq, n_keys=n_keys, qpos0=pos0 + q_start, topk=topk),
        grid=(b, n_q // tq),
        in_specs=[pl.BlockSpec((1, tq, dq), lambda i, j: (i, off + j, 0)),
                  pl.BlockSpec((1, tq, IDX_HEADS), lambda i, j: (i, off + j, 0)),
                  pl.BlockSpec((1, LANES, n_keys), lambda i, j: (i, 0, 0)),
                  pl.BlockSpec((1, LANES, n_keys), lambda i, j: (i, 0, 0))],
        out_specs=pl.BlockSpec((1, tq, n_keys), lambda i, j: (i, j, 0)),
        out_shape=jax.ShapeDtypeStruct((b, n_q, n_keys), BF16),
        scratch_shapes=[pltpu.VMEM((tq, n_keys), F32)],
        compiler_params=_params("parallel", "arbitrary"),
        name="select_topk",
    )(qi, wi, klo, khi)


def _attn_kernel(q_ref, bias_ref, *refs, heads, seg_lens, head_major):
    nseg = len(seg_lens)
    k_refs, v_refs, o_ref = refs[:nseg], refs[nseg:2 * nseg], refs[-1]

    def slab(ref, h, n, strided):
        if strided:
            return ref.reshape(n * heads, HEAD_DIM)[pl.ds(h, n, stride=heads), :].astype(BF16)
        return ref[0, :, h * HEAD_DIM:(h + 1) * HEAD_DIM]

    offs = [sum(seg_lens[:i]) for i in range(nseg)]
    biases = [bias_ref[0, :, o:o + n].astype(F32) for o, n in zip(offs, seg_lens)]
    for h in range(heads):
        sl = slice(h * HEAD_DIM, (h + 1) * HEAD_DIM)
        qh = q_ref[0, :, sl]
        scores = []
        for k_ref, bias, n, hm in zip(k_refs, biases, seg_lens, head_major):
            scores.append(lax.dot_general(qh, slab(k_ref, h, n, hm), (((1,), (1,)), ((), ())),
                                          preferred_element_type=F32) + bias)
        m = scores[0].max(axis=-1, keepdims=True)
        for s in scores[1:]:
            m = jnp.maximum(m, s.max(axis=-1, keepdims=True))
        denom = None
        out = None
        for s, v_ref, n, hm in zip(scores, v_refs, seg_lens, head_major):
            p = jnp.exp(s - m)
            psum = p.sum(axis=-1, keepdims=True)
            pv = jnp.dot(p.astype(BF16), slab(v_ref, h, n, hm), preferred_element_type=F32)
            denom = psum if denom is None else denom + psum
            out = pv if out is None else out + pv
        o_ref[0, :, sl] = (out / denom).astype(o_ref.dtype)


def _attention(q, bias, segs, q_start, heads_per_step, layer=0, out_prev=None):
    b, t, d = q.shape
    n_q = bias.shape[1]
    qb = q_start // n_q
    dh = heads_per_step * HEAD_DIM
    seg_lens = tuple(n for _, _, n in segs)
    head_major = tuple(k.ndim == 5 for k, _, _ in segs)
    assert not any(head_major) or heads_per_step == SUBLANES
    kv_specs = [
        pl.BlockSpec((None, None, n, heads_per_step, HEAD_DIM), lambda i, g: (layer, i, 0, g, 0)) if hm
        else pl.BlockSpec((1, n, dh), lambda i, g: (i, 0, g))
        for n, hm in zip(seg_lens, head_major)]
    in_specs = [pl.BlockSpec((1, n_q, dh), lambda i, g: (i, qb, g)),
                pl.BlockSpec((1, n_q, sum(seg_lens)), lambda i, g: (i, 0, 0))] + kv_specs + kv_specs
    args = [q, bias] + [k for k, _, _ in segs] + [v for _, v, _ in segs]
    aliases = {}
    if out_prev is not None:
        in_specs.append(pl.BlockSpec(memory_space=pl.ANY))
        aliases = {len(args): 0}
        args.append(out_prev)
    return pl.pallas_call(
        functools.partial(_attn_kernel, heads=heads_per_step, seg_lens=seg_lens, head_major=head_major),
        grid=(b, d // dh),
        in_specs=in_specs,
        out_specs=pl.BlockSpec((1, n_q, dh), lambda i, g: (i, qb, g)),
        out_shape=jax.ShapeDtypeStruct((b, t, d), BF16),
        input_output_aliases=aliases,
        compiler_params=_params("parallel", "arbitrary"),
        name="sparse_attention",
    )(*args)


def _postmix_kernel(x_ref, c_ref, a_ref, sg_ref, wc_ref, wa_ref, wo_ref, o_ref):
    d = x_ref.shape[1]
    c = jnp.dot(c_ref[...], wc_ref[...], preferred_element_type=F32)
    a = jnp.dot(a_ref[...], wa_ref[...], preferred_element_type=F32)
    m = sg_ref[:, :d].astype(F32) * c + sg_ref[:, d:].astype(F32) * a
    o_ref[...] = x_ref[...] + jnp.dot(m.astype(BF16), wo_ref[...], preferred_element_type=F32)


def _postmix(x, cact, attn, sg, wc, wa, wo, tm=256):
    m, d = x.shape
    tm = min(tm, m)
    row = lambda width: pl.BlockSpec((tm, width), lambda i: (i, 0))
    full = lambda w: pl.BlockSpec(w.shape, lambda i: (0, 0), pipeline_mode=pl.Buffered(1))
    return pl.pallas_call(
        _postmix_kernel,
        grid=(m // tm,),
        in_specs=[row(d), row(cact.shape[1]), row(attn.shape[1]), row(sg.shape[1]),
                  full(wc), full(wa), full(wo)],
        out_specs=row(d),
        out_shape=jax.ShapeDtypeStruct((m, d), F32),
        compiler_params=_params("parallel"),
        name="postmix",
    )(x, cact, attn, sg, wc, wa, wo)


def _ffn_kernel(x_ref, g_ref, wg_ref, wu_ref, wd_ref, gf_ref, o_ref, h_ref):
    f = pl.program_id(1)

    @pl.when(f == 0)
    def _():
        x = x_ref[...]
        ms = jnp.mean(x * x, axis=-1, keepdims=True)
        h_ref[...] = (x * lax.rsqrt(ms + EPS) * g_ref[...]).astype(h_ref.dtype)
        o_ref[...] = jnp.zeros_like(o_ref)

    h = h_ref[...]
    gate = jnp.dot(h, wg_ref[...], preferred_element_type=F32)
    up = jnp.dot(h, wu_ref[...], preferred_element_type=F32)
    act = (gate * jax.nn.sigmoid(gate) * up).astype(BF16)
    o_ref[...] += jnp.dot(act, wd_ref[...], preferred_element_type=F32)

    @pl.when(f == pl.num_programs(1) - 1)
    def _():
        y = x_ref[...] + o_ref[...]
        ms = jnp.mean(y * y, axis=-1, keepdims=True)
        o_ref[...] = y * lax.rsqrt(ms + EPS) * gf_ref[...]


def _ffn(x, g_ffn, wg, wu, wd, g_final, tm=512, tf=512):
    m, d = x.shape
    dff = wg.shape[1]
    tm = min(tm, m)
    return pl.pallas_call(
        _ffn_kernel,
        grid=(m // tm, dff // tf),
        in_specs=[pl.BlockSpec((tm, d), lambda i, f: (i, 0)),
                  pl.BlockSpec((1, d), lambda i, f: (0, 0)),
                  pl.BlockSpec((d, tf), lambda i, f: (0, f)),
                  pl.BlockSpec((d, tf), lambda i, f: (0, f)),
                  pl.BlockSpec((tf, d), lambda i, f: (f, 0)),
                  pl.BlockSpec((1, d), lambda i, f: (0, 0))],
        out_specs=pl.BlockSpec((tm, d), lambda i, f: (i, 0)),
        out_shape=jax.ShapeDtypeStruct((m, d), F32),
        scratch_shapes=[pltpu.VMEM((tm, d), BF16)],
        compiler_params=_params("parallel", "arbitrary"),
        name="ffn",
    )(x, g_ffn.reshape(1, d), wg, wu, wd, g_final.reshape(1, d))


def _prep_weights(w_in, d_conv, d_attn):
    s = [d_conv, d_conv, d_attn, d_attn, d_attn, IDX_HEADS * IDX_DIM, IDX_DIM, IDX_HEADS]
    pts = [0]
    for n in s:
        pts.append(pts[-1] + n)
    wa, wb, wq, wk, wv, wqi, wki, wwi = [w_in[:, pts[i]:pts[i + 1]] for i in range(8)]
    wgate = w_in[:, pts[8]:]
    d = w_in.shape[0]
    half = min(512, d_conv)
    nt = d_conv // half
    glu = jnp.concatenate([wa.reshape(d, nt, half), wb.reshape(d, nt, half)], axis=2).reshape(d, 2 * d_conv)
    small = jnp.concatenate(
        [wki, wwi, jnp.zeros((d, LANES - IDX_DIM - IDX_HEADS), w_in.dtype)], axis=1)
    cast = lambda w: w.astype(BF16)
    return dict(glu=cast(glu), q=cast(wq), k=cast(wk), v=cast(wv), qi=cast(wqi), small=cast(small),
                gate=cast(wgate))


def _group(x, past_conv, past_k, past_v, past_ki, layer, wts, p, tq, q_class, conv_chunk, heads_per_step):
    b, t, d = x.shape
    q_class = min(q_class, t)
    m = b * t
    x2 = x.reshape(m, d)
    d_conv = p["conv_w"].shape[1]
    d_attn = wts["q"].shape[1]

    h = _rmsnorm(x2, p["g_mix"], BF16)
    half = min(512, d_conv)
    (u,) = _matmul(_glu_kernel, h, wts["glu"], [(half, F32)], tn=2 * half, name="proj_glu")
    (q,) = _matmul(functools.partial(_proj_kernel, scale=HEAD_DIM ** -0.5), h, wts["q"],
                   [(None, BF16)], name="proj_q")
    k32, kbf = _matmul(functools.partial(_proj_kernel, scale=1.0), h, wts["k"],
                       [(None, F32), (None, BF16)], name="proj_k")
    v32, vbf = _matmul(functools.partial(_proj_kernel, scale=1.0), h, wts["v"],
                       [(None, F32), (None, BF16)], name="proj_v")
    (qi,) = _matmul(functools.partial(_proj_kernel, scale=1.0), h, wts["qi"],
                    [(None, BF16)], name="proj_qi")
    ki, wi = _matmul(_idx_small_kernel, h, wts["small"], [(IDX_DIM, F32), (IDX_HEADS, F32)],
                     extra=(p["idx_ln_g"].reshape(1, IDX_DIM), p["idx_ln_b"].reshape(1, IDX_DIM)),
                     name="proj_idx")
    (sg,) = _matmul(_sigmoid_kernel, h, wts["gate"], [(None, BF16)], name="proj_gates")

    u3 = u.reshape(b, t, d_conv)
    ki3 = ki.reshape(b, t, IDX_DIM)
    if past_conv is None:
        cache_pad = jnp.zeros((b, CONV_HALO, d_conv), F32)
        ki_all = ki3
        n_past = 0
    else:
        cache_pad = jnp.pad(past_conv.astype(F32), ((0, 0), (CONV_HALO - (CONV_WIDTH - 1), 0), (0, 0)))
        n_past = past_k.shape[2]
        ki_all = jnp.concatenate([past_ki, ki3], axis=1)
    cact = _conv_branch(u3, cache_pad, p["conv_w"], p["conv_b"], p["conv_ln_g"], p["conv_ln_b"], conv_chunk)

    n_real = n_past + t
    n_keys = -(-n_real // LANES) * LANES
    topk = min(TOPK_MAX, n_real // 4)
    kit = jnp.swapaxes(jnp.pad(ki_all, ((0, 0), (0, n_keys - n_real), (0, 0))), 1, 2).astype(BF16)
    zeros = jnp.zeros_like(kit)
    klo = jnp.concatenate([kit, zeros], axis=1)
    khi = jnp.concatenate([zeros, kit], axis=1)
    qi3 = qi.reshape(b, t, -1)
    wi3 = wi.reshape(b, t, IDX_HEADS)
    q3 = q.reshape(b, t, d_attn)
    kb3 = kbf.reshape(b, t, d_attn)
    vb3 = vbf.reshape(b, t, d_attn)
    if past_k is None:
        attn = None
        for q_start in range(0, t, q_class):
            n_k = q_start + q_class
            bias = _select(qi3, wi3, klo, khi, tq, q_start, q_class, n_k, 0, topk)
            attn = _attention(q3, bias, [(kb3, vb3, n_k)], q_start, heads_per_step, out_prev=attn)
    else:
        bias = _select(qi3, wi3, klo, khi, tq, 0, t, n_keys, n_past, topk)
        pad = ((0, 0), (0, n_keys - n_real), (0, 0))
        segs = [(past_k, past_v, n_past), (jnp.pad(kb3, pad), jnp.pad(vb3, pad), n_keys - n_past)]
        attn = _attention(q3, bias, segs, 0, heads_per_step, layer=layer)

    x1 = _postmix(x2, cact.reshape(m, d_conv), attn.reshape(m, d_attn), sg,
                  p["w_conv_out"], p["w_attn_out"], p["w_out"])
    tail = CONV_WIDTH - 1
    return (x1, u3[:, t - tail:, :], k32.reshape(b, t, N_HEADS, HEAD_DIM),
            v32.reshape(b, t, N_HEADS, HEAD_DIM), ki3)


def kernel(x_prompt, x_sample, cache_conv, cache_k, cache_v, cache_kidx, g_mix, w_in, conv_w, conv_b,
           conv_ln_g, conv_ln_b, w_conv_out, idx_ln_g, idx_ln_b, w_attn_out, w_out, g_ffn, w_gate, w_up,
           w_down, g_final):
    depth = w_in.shape[0]
    assert depth == 1 and x_prompt.shape[1] >= CONV_WIDTH - 1 and x_sample.shape[1] >= CONV_WIDTH - 1
    d_conv = conv_w.shape[2]
    d_attn = w_attn_out.shape[1]
    l = 0
    wts = _prep_weights(w_in[l], d_conv, d_attn)
    p = dict(g_mix=g_mix[l], conv_w=conv_w[l], conv_b=conv_b[l], conv_ln_g=conv_ln_g[l],
             conv_ln_b=conv_ln_b[l], idx_ln_g=idx_ln_g[l], idx_ln_b=idx_ln_b[l],
             w_conv_out=w_conv_out[l].astype(BF16), w_attn_out=w_attn_out[l].astype(BF16),
             w_out=w_out[l].astype(BF16))
    wg, wu, wd = w_gate[l].astype(BF16), w_up[l].astype(BF16), w_down[l].astype(BF16)

    xp, conv_p, k_p, v_p, kidx_p = _group(
        x_prompt, None, None, None, None, l, wts, p, tq=128, q_class=512, conv_chunk=128,
        heads_per_step=8)
    xs, conv_s, k_s, v_s, kidx_s = _group(
        x_sample, cache_conv[l], cache_k, cache_v, cache_kidx[l], l, wts, p,
        tq=x_sample.shape[1], q_class=x_sample.shape[1], conv_chunk=x_sample.shape[1],
        heads_per_step=8)

    y_prompt = _ffn(xp, g_ffn[l], wg, wu, wd, g_final).reshape(x_prompt.shape)
    y_sample = _ffn(xs, g_ffn[l], wg, wu, wd, g_final).reshape(x_sample.shape)
    return (y_prompt, y_sample, conv_p[None], k_p[None], v_p[None], kidx_p[None],
            conv_s[None], k_s[None], v_s[None], kidx_s[None])
```

```python
import functools

import jax
import jax.numpy as jnp
from jax import lax
from jax.experimental import pallas as pl
from jax.experimental.pallas import tpu as pltpu

F32 = jnp.float32
BF16 = jnp.bfloat16

CHUNK = 64
CONV_WIDTH = 31
N_HEADS = 16
HEAD_DIM = 128
IDX_HEADS = 16
IDX_DIM = 64
TOPK_MAX = 256
EPS = 1e-6

LANES = 128
SUBLANES = 8
CONV_HALO = 32
CONV_ROWS = 32
SEARCH_UNROLL = 4
PROJ_TN = 1024
FFN_TF = 512
VMEM_LIMIT = 56 * 2**20


def _params(*sem):
    return pltpu.CompilerParams(dimension_semantics=sem, vmem_limit_bytes=VMEM_LIMIT)


def _rms_kernel(x_ref, g_ref, o_ref):
    x = x_ref[...]
    ms = jnp.mean(x * x, axis=-1, keepdims=True)
    o_ref[...] = (x * lax.rsqrt(ms + EPS) * g_ref[...]).astype(o_ref.dtype)


def _rmsnorm(x, g, out_dtype, tm=512):
    m, d = x.shape
    tm = min(tm, m)
    return pl.pallas_call(
        _rms_kernel,
        grid=(m // tm,),
        in_specs=[pl.BlockSpec((tm, d), lambda i: (i, 0)),
                  pl.BlockSpec((1, d), lambda i: (0, 0))],
        out_specs=pl.BlockSpec((tm, d), lambda i: (i, 0)),
        out_shape=jax.ShapeDtypeStruct((m, d), out_dtype),
        compiler_params=_params("parallel"),
        name="rmsnorm",
    )(x, g.reshape(1, d))


def _proj_kernel(h_ref, w_ref, *io_refs, epilogue):
    epilogue(jnp.dot(h_ref[...], w_ref[...], preferred_element_type=F32), *io_refs)


def _epi_glu(z, u_ref):
    half = z.shape[1] // 2
    u_ref[...] = z[:, :half] * jax.nn.sigmoid(z[:, half:])


def _epi_query(z, q_ref):
    q_ref[...] = (z * HEAD_DIM ** -0.5).astype(q_ref.dtype)


def _epi_sigmoid(z, o_ref):
    o_ref[...] = jax.nn.sigmoid(z).astype(o_ref.dtype)


def _epi_key_value(z, o32_ref, obf_ref):
    tm, heads = z.shape[0], o32_ref.shape[1]
    flat = o32_ref.reshape(tm * heads, HEAD_DIM)
    for h in range(heads):
        flat[pl.ds(h, tm, stride=heads), :] = z[:, h * HEAD_DIM:(h + 1) * HEAD_DIM]
    obf_ref[...] = z.astype(obf_ref.dtype)


def _epi_indexer(z, g_ref, b_ref, qi_ref, ki_ref, wi_ref):
    nq = qi_ref.shape[1]
    qi_ref[...] = z[:, :nq].astype(qi_ref.dtype)
    kz = z[:, nq:nq + IDX_DIM]
    mu = jnp.mean(kz, axis=-1, keepdims=True)
    var = jnp.mean(jnp.square(kz - mu), axis=-1, keepdims=True)
    ki_ref[...] = (kz - mu) * lax.rsqrt(var + EPS) * g_ref[...] + b_ref[...]
    wi_ref[...] = z[:, nq + IDX_DIM:nq + IDX_DIM + IDX_HEADS]


def _proj(h, w, tn, epilogue, outs, extra=(), tm=1024, name="proj"):
    m, k = h.shape
    nt = w.shape[1] // tn
    tm = min(tm, m)

    def out_spec(block_tail):
        pad = (0,) * (len(block_tail) - 1)
        return pl.BlockSpec((tm,) + block_tail, lambda i, j: (i, j) + pad)

    in_specs = [pl.BlockSpec((tm, k), lambda i, j: (i, 0)),
                pl.BlockSpec((k, tn), lambda i, j: (0, j))]
    in_specs += [pl.BlockSpec(e.shape, lambda i, j: (0, 0)) for e in extra]
    return pl.pallas_call(
        functools.partial(_proj_kernel, epilogue=epilogue),
        grid=(m // tm, nt),
        in_specs=in_specs,
        out_specs=[out_spec(bt) for bt, _ in outs],
        out_shape=[jax.ShapeDtypeStruct((m, nt * bt[0]) + bt[1:], dt) for bt, dt in outs],
        compiler_params=_params("parallel", "arbitrary"),
        name=name,
    )(h, w, *extra)


def _conv_kernel(cur_ref, prev_ref, cache_ref, cw_ref, cb_ref, g_ref, b_ref, o_ref, ext_ref, *, tc):
    c = pl.program_id(1)
    rows = CONV_HALO + tc
    ext_ref[0, 0:CONV_HALO, :] = jnp.where(c == 0, cache_ref[0], prev_ref[0])
    ext_ref[0, CONV_HALO:rows, :] = cur_ref[0]
    for r in range(1, SUBLANES):
        ext_ref[r, 0:rows - SUBLANES, :] = ext_ref[0, r:r + rows - SUBLANES, :]
    shift = CONV_HALO - (CONV_WIDTH - 1)

    for base in range(0, tc, CONV_ROWS):
        acc = jnp.broadcast_to(cb_ref[...], (CONV_ROWS, cb_ref.shape[1]))
        for w in range(CONV_WIDTH):
            r = (shift + w) % SUBLANES
            start = base + shift + w - r
            acc = acc + ext_ref[r, start:start + CONV_ROWS, :] * cw_ref[w:w + 1, :]
        mu = jnp.mean(acc, axis=-1, keepdims=True)
        var = jnp.mean(jnp.square(acc - mu), axis=-1, keepdims=True)
        y = (acc - mu) * lax.rsqrt(var + EPS) * g_ref[...] + b_ref[...]
        o_ref[0, base:base + CONV_ROWS, :] = (y * jax.nn.sigmoid(y)).astype(o_ref.dtype)


def _conv_branch(u, cache_pad, conv_w, conv_b, ln_g, ln_b, tc):
    b, t, c = u.shape
    tc = min(tc, t)
    per = tc // CONV_HALO
    return pl.pallas_call(
        functools.partial(_conv_kernel, tc=tc),
        grid=(b, t // tc),
        in_specs=[pl.BlockSpec((1, tc, c), lambda i, j: (i, j, 0)),
                  pl.BlockSpec((1, CONV_HALO, c), lambda i, j: (i, jnp.maximum(j * per - 1, 0), 0)),
                  pl.BlockSpec((1, CONV_HALO, c), lambda i, j: (i, 0, 0)),
                  pl.BlockSpec((CONV_WIDTH, c), lambda i, j: (0, 0)),
                  pl.BlockSpec((1, c), lambda i, j: (0, 0)),
                  pl.BlockSpec((1, c), lambda i, j: (0, 0)),
                  pl.BlockSpec((1, c), lambda i, j: (0, 0))],
        out_specs=pl.BlockSpec((1, tc, c), lambda i, j: (i, j, 0)),
        out_shape=jax.ShapeDtypeStruct((b, t, c), BF16),
        scratch_shapes=[pltpu.VMEM((SUBLANES, CONV_HALO + tc, c), F32)],
        compiler_params=_params("parallel", "arbitrary"),
        name="conv_branch",
    )(u, u, cache_pad, conv_w, conv_b.reshape(1, c), ln_g.reshape(1, c), ln_b.reshape(1, c))


def _select_kernel(qi_ref, wi_ref, klo_ref, khi_ref, bias_ref, score_ref, *, tq, n_keys, qpos0, topk):
    j = pl.program_id(1)
    wi = wi_ref[0] * (IDX_HEADS ** -0.5 * IDX_DIM ** -0.5)
    acc = None
    for p in range(IDX_HEADS // 2):
        qt = qi_ref[0, :, p * LANES:(p + 1) * LANES]
        for head, k_ref in ((2 * p, klo_ref), (2 * p + 1, khi_ref)):
            logits = jnp.dot(qt, k_ref[0], preferred_element_type=F32)
            term = wi[:, head:head + 1] * jnp.maximum(logits, 0.0)
            acc = term if acc is None else acc + term

    kpos = lax.broadcasted_iota(jnp.int32, (tq, n_keys), 1)
    qpos = qpos0 + j * tq + lax.broadcasted_iota(jnp.int32, (tq, 1), 0)
    limit = (qpos // CHUNK + 1) * CHUNK
    adm = kpos < limit
    score_ref[...] = jnp.where(adm, acc, -jnp.inf)
    kf = float(topk)
    searching = jnp.minimum(limit, n_keys) > topk
    lo0 = jnp.min(jnp.where(adm, acc, jnp.inf), axis=1, keepdims=True)
    hi0 = jnp.max(score_ref[...], axis=1, keepdims=True)
    lo0 = jnp.where(searching, lo0, -jnp.inf)
    hi0 = jnp.where(searching, hi0, -jnp.inf)

    def count_ge(t):
        return jnp.sum(jnp.where(score_ref[...] >= t, 1.0, 0.0), axis=1, keepdims=True)

    def step(lo, hi):
        mid = 0.5 * lo + 0.5 * hi
        cnt = count_ge(mid)
        return jnp.where(cnt >= kf, mid, lo), jnp.where(cnt > kf, hi, mid)

    def unresolved(lo, hi):
        mid = 0.5 * lo + 0.5 * hi
        inside = jnp.logical_and(mid > lo, mid < hi)
        return jnp.max(jnp.where(inside, 1.0, 0.0))

    def body(carry):
        lo, hi, _ = carry
        for _ in range(SEARCH_UNROLL):
            lo, hi = step(lo, hi)
        return lo, hi, unresolved(lo, hi)

    lo, hi, _ = lax.while_loop(lambda carry: carry[2] > 0.0, body, (lo0, hi0, unresolved(lo0, hi0)))

    score = score_ref[...]
    bias_ref[0] = jnp.where(jnp.logical_and(score >= lo, adm), 0.0, -jnp.inf).astype(bias_ref.dtype)

    tie = jnp.where(jnp.logical_and(searching, count_ge(lo) != kf), 1.0, 0.0)

    @pl.when(jnp.max(tie) > 0.0)
    def _():
        score = score_ref[...]
        t = jnp.where(count_ge(hi) >= kf, hi, lo)
        gt = score > t
        eq = score == t
        need = kf - jnp.sum(jnp.where(gt, 1.0, 0.0), axis=1, keepdims=True)
        r = lax.broadcasted_iota(jnp.int32, (LANES, LANES), 0)
        cidx = lax.broadcasted_iota(jnp.int32, (LANES, LANES), 1)
        tri = jnp.where(r <= cidx, 1.0, 0.0).astype(BF16)
        carry = jnp.zeros((tq, 1), F32)
        for s in range(n_keys // LANES):
            sl = slice(s * LANES, (s + 1) * LANES)
            e = jnp.where(eq[:, sl], 1.0, 0.0)
            pre = jnp.dot(e.astype(BF16), tri, preferred_element_type=F32) + carry
            carry = pre[:, LANES - 1:LANES]
            keep = jnp.logical_or(gt[:, sl], jnp.logical_and(eq[:, sl], pre <= need))
            plain = jnp.logical_and(score[:, sl] >= lo, adm[:, sl])
            bias = jnp.where(tie > 0.0, jnp.where(keep, 0.0, -jnp.inf), jnp.where(plain, 0.0, -jnp.inf))
            bias_ref[0, :, sl] = bias.astype(bias_ref.dtype)


def _select(qi, wi, klo, khi, tq, q_start, n_q, n_keys, pos0, topk):
    b, _, dq = qi.shape
    off = q_start // tq
    return pl.pallas_call(
        functools.partial(_select_kernel, tq=tq, n_keys=n_keys, qpos0=pos0 + q_start, topk=topk),
        grid=(b, n_q // tq),
        in_specs=[pl.BlockSpec((1, tq, dq), lambda i, j: (i, off + j, 0)),
                  pl.BlockSpec((1, tq, IDX_HEADS), lambda i, j: (i, off + j, 0)),
                  pl.BlockSpec((1, LANES, n_keys), lambda i, j: (i, 0, 0)),
                  pl.BlockSpec((1, LANES, n_keys), lambda i, j: (i, 0, 0))],
        out_specs=pl.BlockSpec((1, tq, n_keys), lambda i, j: (i, j, 0)),
        out_shape=jax.ShapeDtypeStruct((b, n_q, n_keys), BF16),
        scratch_shapes=[pltpu.VMEM((tq, n_keys), F32)],
        compiler_params=_params("parallel", "arbitrary"),
        name="select_topk",
    )(qi, wi, klo, khi)


def _attn_kernel(q_ref, bias_ref, *refs, heads, seg_lens, head_major):
    nseg = len(seg_lens)
    k_refs, v_refs, o_ref = refs[:nseg], refs[nseg:2 * nseg], refs[-1]

    def slab(ref, h, n, strided):
        if strided:
            return ref.reshape(n * heads, HEAD_DIM)[pl.ds(h, n, stride=heads), :].astype(BF16)
        return ref[0, :, h * HEAD_DIM:(h + 1) * HEAD_DIM]

    offs = [sum(seg_lens[:i]) for i in range(nseg)]
    biases = [bias_ref[0, :, o:o + n].astype(F32) for o, n in zip(offs, seg_lens)]
    for h in range(heads):
        sl = slice(h * HEAD_DIM, (h + 1) * HEAD_DIM)
        qh = q_ref[0, :, sl]
        scores = []
        for k_ref, bias, n, hm in zip(k_refs, biases, seg_lens, head_major):
            scores.append(lax.dot_general(qh, slab(k_ref, h, n, hm), (((1,), (1,)), ((), ())),
                                          preferred_element_type=F32) + bias)
        m = scores[0].max(axis=-1, keepdims=True)
        for s in scores[1:]:
            m = jnp.maximum(m, s.max(axis=-1, keepdims=True))
        denom = None
        out = None
        for s, v_ref, n, hm in zip(scores, v_refs, seg_lens, head_major):
            p = jnp.exp(s - m)
            psum = p.sum(axis=-1, keepdims=True)
            pv = jnp.dot(p.astype(BF16), slab(v_ref, h, n, hm), preferred_element_type=F32)
            denom = psum if denom is None else denom + psum
            out = pv if out is None else out + pv
        o_ref[0, :, sl] = (out / denom).astype(o_ref.dtype)


def _attention(q, bias, segs, q_start, heads_per_step, layer=0, out_prev=None):
    b, t, d = q.shape
    n_q = bias.shape[1]
    qb = q_start // n_q
    dh = heads_per_step * HEAD_DIM
    seg_lens = tuple(n for _, _, n in segs)
    head_major = tuple(k.ndim == 5 for k, _, _ in segs)
    assert not any(head_major) or heads_per_step == SUBLANES
    kv_specs = [
        pl.BlockSpec((None, None, n, heads_per_step, HEAD_DIM), lambda i, g: (layer, i, 0, g, 0)) if hm
        else pl.BlockSpec((1, n, dh), lambda i, g: (i, 0, g))
        for n, hm in zip(seg_lens, head_major)]
    in_specs = [pl.BlockSpec((1, n_q, dh), lambda i, g: (i, qb, g)),
                pl.BlockSpec((1, n_q, sum(seg_lens)), lambda i, g: (i, 0, 0))] + kv_specs + kv_specs
    args = [q, bias] + [k for k, _, _ in segs] + [v for _, v, _ in segs]
    aliases = {}
    if out_prev is not None:
        in_specs.append(pl.BlockSpec(memory_space=pl.ANY))
        aliases = {len(args): 0}
        args.append(out_prev)
    return pl.pallas_call(
        functools.partial(_attn_kernel, heads=heads_per_step, seg_lens=seg_lens, head_major=head_major),
        grid=(b, d // dh),
        in_specs=in_specs,
        out_specs=pl.BlockSpec((1, n_q, dh), lambda i, g: (i, qb, g)),
        out_shape=jax.ShapeDtypeStruct((b, t, d), BF16),
        input_output_aliases=aliases,
        compiler_params=_params("parallel", "arbitrary"),
        name="sparse_attention",
    )(*args)


def _postmix_kernel(x_ref, c_ref, a_ref, sg_ref, wc_ref, wa_ref, wo_ref, o_ref):
    d = x_ref.shape[1]
    c = jnp.dot(c_ref[...], wc_ref[...], preferred_element_type=F32)
    a = jnp.dot(a_ref[...], wa_ref[...], preferred_element_type=F32)
    m = sg_ref[:, :d].astype(F32) * c + sg_ref[:, d:].astype(F32) * a
    o_ref[...] = x_ref[...] + jnp.dot(m.astype(BF16), wo_ref[...], preferred_element_type=F32)


def _postmix(x, cact, attn, sg, wc, wa, wo, tm=256):
    m, d = x.shape
    tm = min(tm, m)
    row = lambda width: pl.BlockSpec((tm, width), lambda i: (i, 0))
    full = lambda w: pl.BlockSpec(w.shape, lambda i: (0, 0), pipeline_mode=pl.Buffered(1))
    return pl.pallas_call(
        _postmix_kernel,
        grid=(m // tm,),
        in_specs=[row(d), row(cact.shape[1]), row(attn.shape[1]), row(sg.shape[1]),
                  full(wc), full(wa), full(wo)],
        out_specs=row(d),
        out_shape=jax.ShapeDtypeStruct((m, d), F32),
        compiler_params=_params("parallel"),
        name="postmix",
    )(x, cact, attn, sg, wc, wa, wo)


def _ffn_kernel(x_ref, g_ref, wg_ref, wu_ref, wd_ref, gf_ref, o_ref, h_ref):
    f = pl.program_id(1)

    @pl.when(f == 0)
    def _():
        x = x_ref[...]
        ms = jnp.mean(x * x, axis=-1, keepdims=True)
        h_ref[...] = (x * lax.rsqrt(ms + EPS) * g_ref[...]).astype(h_ref.dtype)
        o_ref[...] = jnp.zeros_like(o_ref)

    h = h_ref[...]
    gate = jnp.dot(h, wg_ref[...], preferred_element_type=F32)
    up = jnp.dot(h, wu_ref[...], preferred_element_type=F32)
    act = (gate * jax.nn.sigmoid(gate) * up).astype(BF16)
    o_ref[...] += jnp.dot(act, wd_ref[...], preferred_element_type=F32)

    @pl.when(f == pl.num_programs(1) - 1)
    def _():
        y = x_ref[...] + o_ref[...]
        ms = jnp.mean(y * y, axis=-1, keepdims=True)
        o_ref[...] = y * lax.rsqrt(ms + EPS) * gf_ref[...]


def _ffn(x, g_ffn, wg, wu, wd, g_final, tm=512):
    m, d = x.shape
    dff = wg.shape[1]
    tf = min(FFN_TF, dff)
    tm = min(tm, m)
    return pl.pallas_call(
        _ffn_kernel,
        grid=(m // tm, dff // tf),
        in_specs=[pl.BlockSpec((tm, d), lambda i, f: (i, 0)),
                  pl.BlockSpec((1, d), lambda i, f: (0, 0)),
                  pl.BlockSpec((d, tf), lambda i, f: (0, f)),
                  pl.BlockSpec((d, tf), lambda i, f: (0, f)),
                  pl.BlockSpec((tf, d), lambda i, f: (f, 0)),
                  pl.BlockSpec((1, d), lambda i, f: (0, 0))],
        out_specs=pl.BlockSpec((tm, d), lambda i, f: (i, 0)),
        out_shape=jax.ShapeDtypeStruct((m, d), F32),
        scratch_shapes=[pltpu.VMEM((tm, d), BF16)],
        compiler_params=_params("parallel", "arbitrary"),
        name="ffn",
    )(x, g_ffn.reshape(1, d), wg, wu, wd, g_final.reshape(1, d))


def _prep_weights(w_in, d_conv, d_attn):
    s = [d_conv, d_conv, d_attn, d_attn, d_attn, IDX_HEADS * IDX_DIM, IDX_DIM, IDX_HEADS]
    pts = [0]
    for n in s:
        pts.append(pts[-1] + n)
    wa, wb, wq, wk, wv, wqi, wki, wwi = [w_in[:, pts[i]:pts[i + 1]] for i in range(8)]
    wgate = w_in[:, pts[8]:]
    d = w_in.shape[0]
    tn_glu = min(PROJ_TN, 2 * d_conv)
    half = tn_glu // 2
    glu = jnp.concatenate([wa.reshape(d, -1, half), wb.reshape(d, -1, half)], axis=2).reshape(d, 2 * d_conv)
    pad = jnp.zeros((d, LANES - IDX_DIM - IDX_HEADS), w_in.dtype)
    idx = jnp.concatenate([wqi, wki, wwi, pad], axis=1)
    cast = lambda w: w.astype(BF16)
    return dict(glu=cast(glu), tn_glu=tn_glu, q=cast(wq), k=cast(wk), v=cast(wv), gate=cast(wgate),
                idx=cast(idx))


def _group(x, past_conv, past_k, past_v, past_ki, layer, wts, p, tq, q_class, conv_chunk, heads_per_step):
    b, t, d = x.shape
    q_class = min(q_class, t)
    m = b * t
    x2 = x.reshape(m, d)
    d_conv = p["conv_w"].shape[1]
    d_attn = N_HEADS * HEAD_DIM
    h = _rmsnorm(x2, p["g_mix"], BF16)
    tn_glu = wts["tn_glu"]
    (u,) = _proj(h, wts["glu"], tn_glu, _epi_glu, [((tn_glu // 2,), F32)], name="proj_glu")
    (q,) = _proj(h, wts["q"], PROJ_TN, _epi_query, [((PROJ_TN,), BF16)], name="proj_q")
    hp = PROJ_TN // HEAD_DIM
    kv_outs = [((hp, HEAD_DIM), F32), ((PROJ_TN,), BF16)]
    k32, kbf = _proj(h, wts["k"], PROJ_TN, _epi_key_value, kv_outs, name="proj_k")
    v32, vbf = _proj(h, wts["v"], PROJ_TN, _epi_key_value, kv_outs, name="proj_v")
    tn_gate = min(PROJ_TN, wts["gate"].shape[1])
    (sg,) = _proj(h, wts["gate"], tn_gate, _epi_sigmoid, [((tn_gate,), BF16)], name="proj_gates")
    qi, ki, wi = _proj(
        h, wts["idx"], wts["idx"].shape[1], _epi_indexer,
        [((IDX_HEADS * IDX_DIM,), BF16), ((IDX_DIM,), F32), ((IDX_HEADS,), F32)],
        extra=(p["idx_ln_g"].reshape(1, IDX_DIM), p["idx_ln_b"].reshape(1, IDX_DIM)),
        name="proj_indexer")

    u3 = u.reshape(b, t, d_conv)
    ki3 = ki.reshape(b, t, IDX_DIM)
    if past_conv is None:
        cache_pad = jnp.zeros((b, CONV_HALO, d_conv), F32)
        ki_all = ki3
        n_past = 0
    else:
        cache_pad = jnp.pad(past_conv.astype(F32), ((0, 0), (CONV_HALO - (CONV_WIDTH - 1), 0), (0, 0)))
        n_past = past_k.shape[2]
        ki_all = jnp.concatenate([past_ki, ki3], axis=1)
    cact = _conv_branch(u3, cache_pad, p["conv_w"], p["conv_b"], p["conv_ln_g"], p["conv_ln_b"], conv_chunk)

    n_real = n_past + t
    n_keys = -(-n_real // LANES) * LANES
    topk = min(TOPK_MAX, n_real // 4)
    kit = jnp.swapaxes(jnp.pad(ki_all, ((0, 0), (0, n_keys - n_real), (0, 0))), 1, 2).astype(BF16)
    zeros = jnp.zeros_like(kit)
    klo = jnp.concatenate([kit, zeros], axis=1)
    khi = jnp.concatenate([zeros, kit], axis=1)
    qi3 = qi.reshape(b, t, -1)
    wi3 = wi.reshape(b, t, IDX_HEADS)
    q3 = q.reshape(b, t, d_attn)
    kb3 = kbf.reshape(b, t, d_attn)
    vb3 = vbf.reshape(b, t, d_attn)
    if past_k is None:
        attn = jnp.zeros((b, t, d_attn), BF16)
        for q_start in range(0, t, q_class):
            n_k = q_start + q_class
            bias = _select(qi3, wi3, klo, khi, tq, q_start, q_class, n_k, 0, topk)
            attn = _attention(q3, bias, [(kb3, vb3, n_k)], q_start, heads_per_step, out_prev=attn)
    else:
        bias = _select(qi3, wi3, klo, khi, tq, 0, t, n_keys, n_past, topk)
        pad = ((0, 0), (0, n_keys - n_real), (0, 0))
        segs = [(past_k, past_v, n_past), (jnp.pad(kb3, pad), jnp.pad(vb3, pad), n_keys - n_past)]
        attn = _attention(q3, bias, segs, 0, heads_per_step, layer=layer)

    x1 = _postmix(x2, cact.reshape(m, d_conv), attn.reshape(m, d_attn), sg,
                  p["w_conv_out"], p["w_attn_out"], p["w_out"])
    tail = CONV_WIDTH - 1
    return (x1, u3[:, t - tail:, :], k32.reshape(b, t, N_HEADS, HEAD_DIM),
            v32.reshape(b, t, N_HEADS, HEAD_DIM), ki3)


def kernel(x_prompt, x_sample, cache_conv, cache_k, cache_v, cache_kidx, g_mix, w_in, conv_w, conv_b,
           conv_ln_g, conv_ln_b, w_conv_out, idx_ln_g, idx_ln_b, w_attn_out, w_out, g_ffn, w_gate, w_up,
           w_down, g_final):
    depth = w_in.shape[0]
    assert depth == 1 and x_prompt.shape[1] >= CONV_WIDTH - 1 and x_sample.shape[1] >= CONV_WIDTH - 1
    d_conv = conv_w.shape[2]
    d_attn = w_attn_out.shape[1]
    l = 0
    wts = _prep_weights(w_in[l], d_conv, d_attn)
    p = dict(g_mix=g_mix[l], conv_w=conv_w[l], conv_b=conv_b[l], conv_ln_g=conv_ln_g[l],
             conv_ln_b=conv_ln_b[l], idx_ln_g=idx_ln_g[l], idx_ln_b=idx_ln_b[l],
             w_conv_out=w_conv_out[l].astype(BF16), w_attn_out=w_attn_out[l].astype(BF16),
             w_out=w_out[l].astype(BF16))
    wg, wu, wd = w_gate[l].astype(BF16), w_up[l].astype(BF16), w_down[l].astype(BF16)

    xp, conv_p, k_p, v_p, kidx_p = _group(
        x_prompt, None, None, None, None, l, wts, p, tq=128, q_class=512, conv_chunk=128,
        heads_per_step=8)
    xs, conv_s, k_s, v_s, kidx_s = _group(
        x_sample, cache_conv[l], cache_k, cache_v, cache_kidx[l], l, wts, p,
        tq=x_sample.shape[1], q_class=x_sample.shape[1], conv_chunk=x_sample.shape[1],
        heads_per_step=8)

    y_prompt = _ffn(xp, g_ffn[l], wg, wu, wd, g_final).reshape(x_prompt.shape)
    y_sample = _ffn(xs, g_ffn[l], wg, wu, wd, g_final).reshape(x_sample.shape)
    return (y_prompt, y_sample, conv_p[None], k_p[None], v_p[None], kidx_p[None],
            conv_s[None], k_s[None], v_s[None], kidx_s[None])
```

```python
import functools

import jax
import jax.numpy as jnp
from jax import lax
from jax.experimental import pallas as pl
from jax.experimental.pallas import tpu as pltpu

F32 = jnp.float32
BF16 = jnp.bfloat16

CHUNK = 64
CONV_WIDTH = 31
N_HEADS = 16
HEAD_DIM = 128
IDX_HEADS = 16
IDX_DIM = 64
TOPK_MAX = 256
EPS = 1e-6
LOG2_E = 1.4426950408889634

LANES = 128
SUBLANES = 8
CONV_HALO = 32
CONV_ROWS = 32
SEARCH_UNROLL = 4
PROJ_TN = 1024
FFN_TF = 512
VMEM_LIMIT = 56 * 2**20


def _params(*sem):
    return pltpu.CompilerParams(dimension_semantics=sem, vmem_limit_bytes=VMEM_LIMIT)


def _rms_kernel(x_ref, g_ref, o_ref):
    x = x_ref[...]
    ms = jnp.mean(x * x, axis=-1, keepdims=True)
    o_ref[...] = (x * lax.rsqrt(ms + EPS) * g_ref[...]).astype(o_ref.dtype)


def _rmsnorm(x, g, out_dtype, tm=512):
    m, d = x.shape
    tm = min(tm, m)
    return pl.pallas_call(
        _rms_kernel,
        grid=(m // tm,),
        in_specs=[pl.BlockSpec((tm, d), lambda i: (i, 0)),
                  pl.BlockSpec((1, d), lambda i: (0, 0))],
        out_specs=pl.BlockSpec((tm, d), lambda i: (i, 0)),
        out_shape=jax.ShapeDtypeStruct((m, d), out_dtype),
        compiler_params=_params("parallel"),
        name="rmsnorm",
    )(x, g.reshape(1, d))


def _proj_kernel(h_ref, w_ref, *io_refs, epilogue):
    epilogue(jnp.dot(h_ref[...], w_ref[...], preferred_element_type=F32), *io_refs)


def _epi_glu(z, u_ref):
    half = z.shape[1] // 2
    u_ref[...] = z[:, :half] * jax.nn.sigmoid(z[:, half:])


def _epi_query(z, q_ref):
    q_ref[...] = (z * (HEAD_DIM ** -0.5 * LOG2_E)).astype(q_ref.dtype)


def _epi_sigmoid(z, o_ref):
    o_ref[...] = jax.nn.sigmoid(z).astype(o_ref.dtype)


def _store_head_major(z, o32_ref):
    tm, heads = z.shape[0], o32_ref.shape[1]
    flat = o32_ref.reshape(tm * heads, HEAD_DIM)
    for h in range(heads):
        flat[pl.ds(h, tm, stride=heads), :] = z[:, h * HEAD_DIM:(h + 1) * HEAD_DIM]


def _epi_key(z, o32_ref, obf_ref):
    _store_head_major(z, o32_ref)
    obf_ref[...] = z.astype(obf_ref.dtype)


def _epi_value(z, o32_ref, oext_ref):
    _store_head_major(z, o32_ref)
    ones = jnp.ones((z.shape[0], HEAD_DIM), oext_ref.dtype)
    for h in range(o32_ref.shape[1]):
        oext_ref[:, 2 * h * HEAD_DIM:(2 * h + 1) * HEAD_DIM] = (
            z[:, h * HEAD_DIM:(h + 1) * HEAD_DIM].astype(oext_ref.dtype))
        oext_ref[:, (2 * h + 1) * HEAD_DIM:(2 * h + 2) * HEAD_DIM] = ones


def _epi_indexer(z, g_ref, b_ref, qi_ref, ki_ref, wi_ref):
    nq = qi_ref.shape[1]
    qi_ref[...] = z[:, :nq].astype(qi_ref.dtype)
    kz = z[:, nq:nq + IDX_DIM]
    mu = jnp.mean(kz, axis=-1, keepdims=True)
    var = jnp.mean(jnp.square(kz - mu), axis=-1, keepdims=True)
    ki_ref[...] = (kz - mu) * lax.rsqrt(var + EPS) * g_ref[...] + b_ref[...]
    wi_ref[...] = z[:, nq + IDX_DIM:nq + IDX_DIM + IDX_HEADS]


def _proj(h, w, tn, epilogue, outs, extra=(), tm=1024, name="proj"):
    m, k = h.shape
    nt = w.shape[1] // tn
    tm = min(tm, m)

    def out_spec(block_tail):
        pad = (0,) * (len(block_tail) - 1)
        return pl.BlockSpec((tm,) + block_tail, lambda i, j: (i, j) + pad)

    in_specs = [pl.BlockSpec((tm, k), lambda i, j: (i, 0)),
                pl.BlockSpec((k, tn), lambda i, j: (0, j))]
    in_specs += [pl.BlockSpec(e.shape, lambda i, j: (0, 0)) for e in extra]
    return pl.pallas_call(
        functools.partial(_proj_kernel, epilogue=epilogue),
        grid=(m // tm, nt),
        in_specs=in_specs,
        out_specs=[out_spec(bt) for bt, _ in outs],
        out_shape=[jax.ShapeDtypeStruct((m, nt * bt[0]) + bt[1:], dt) for bt, dt in outs],
        compiler_params=_params("parallel", "arbitrary"),
        name=name,
    )(h, w, *extra)


def _conv_kernel(cur_ref, prev_ref, cache_ref, cw_ref, cb_ref, g_ref, b_ref, o_ref, ext_ref, *, tc):
    c = pl.program_id(1)
    rows = CONV_HALO + tc
    ext_ref[0, 0:CONV_HALO, :] = jnp.where(c == 0, cache_ref[0], prev_ref[0])
    ext_ref[0, CONV_HALO:rows, :] = cur_ref[0]
    for r in range(1, SUBLANES):
        ext_ref[r, 0:rows - SUBLANES, :] = ext_ref[0, r:r + rows - SUBLANES, :]
    shift = CONV_HALO - (CONV_WIDTH - 1)

    for base in range(0, tc, CONV_ROWS):
        acc = jnp.broadcast_to(cb_ref[...], (CONV_ROWS, cb_ref.shape[1]))
        for w in range(CONV_WIDTH):
            r = (shift + w) % SUBLANES
            start = base + shift + w - r
            acc = acc + ext_ref[r, start:start + CONV_ROWS, :] * cw_ref[w:w + 1, :]
        mu = jnp.mean(acc, axis=-1, keepdims=True)
        var = jnp.mean(jnp.square(acc - mu), axis=-1, keepdims=True)
        y = (acc - mu) * lax.rsqrt(var + EPS) * g_ref[...] + b_ref[...]
        o_ref[0, base:base + CONV_ROWS, :] = (y * jax.nn.sigmoid(y)).astype(o_ref.dtype)


def _conv_branch(u, cache_pad, conv_w, conv_b, ln_g, ln_b, tc):
    b, t, c = u.shape
    tc = min(tc, t)
    per = tc // CONV_HALO
    return pl.pallas_call(
        functools.partial(_conv_kernel, tc=tc),
        grid=(b, t // tc),
        in_specs=[pl.BlockSpec((1, tc, c), lambda i, j: (i, j, 0)),
                  pl.BlockSpec((1, CONV_HALO, c), lambda i, j: (i, jnp.maximum(j * per - 1, 0), 0)),
                  pl.BlockSpec((1, CONV_HALO, c), lambda i, j: (i, 0, 0)),
                  pl.BlockSpec((CONV_WIDTH, c), lambda i, j: (0, 0)),
                  pl.BlockSpec((1, c), lambda i, j: (0, 0)),
                  pl.BlockSpec((1, c), lambda i, j: (0, 0)),
                  pl.BlockSpec((1, c), lambda i, j: (0, 0))],
        out_specs=pl.BlockSpec((1, tc, c), lambda i, j: (i, j, 0)),
        out_shape=jax.ShapeDtypeStruct((b, t, c), BF16),
        scratch_shapes=[pltpu.VMEM((SUBLANES, CONV_HALO + tc, c), F32)],
        compiler_params=_params("parallel", "arbitrary"),
        name="conv_branch",
    )(u, u, cache_pad, conv_w, conv_b.reshape(1, c), ln_g.reshape(1, c), ln_b.reshape(1, c))


def _select_kernel(qi_ref, wi_ref, klo_ref, khi_ref, bias_ref, score_ref, *, tq, n_keys, qpos0, topk):
    j = pl.program_id(1)
    wi = wi_ref[0] * (IDX_HEADS ** -0.5 * IDX_DIM ** -0.5)
    acc = None
    for p in range(IDX_HEADS // 2):
        qt = qi_ref[0, :, p * LANES:(p + 1) * LANES]
        for head, k_ref in ((2 * p, klo_ref), (2 * p + 1, khi_ref)):
            logits = jnp.dot(qt, k_ref[0], preferred_element_type=F32)
            term = wi[:, head:head + 1] * jnp.maximum(logits, 0.0)
            acc = term if acc is None else acc + term

    kpos = lax.broadcasted_iota(jnp.int32, (tq, n_keys), 1)
    qpos = qpos0 + j * tq + lax.broadcasted_iota(jnp.int32, (tq, 1), 0)
    limit = (qpos // CHUNK + 1) * CHUNK
    adm = kpos < limit
    score_ref[...] = jnp.where(adm, acc, -jnp.inf)
    kf = float(topk)
    searching = jnp.minimum(limit, n_keys) > topk
    lo0 = jnp.min(jnp.where(adm, acc, jnp.inf), axis=1, keepdims=True)
    hi0 = jnp.max(score_ref[...], axis=1, keepdims=True)
    lo0 = jnp.where(searching, lo0, -jnp.inf)
    hi0 = jnp.where(searching, hi0, -jnp.inf)

    def count_ge(t):
        return jnp.sum(jnp.where(score_ref[...] >= t, 1.0, 0.0), axis=1, keepdims=True)

    def step(lo, hi):
        mid = 0.5 * lo + 0.5 * hi
        cnt = count_ge(mid)
        return jnp.where(cnt >= kf, mid, lo), jnp.where(cnt > kf, hi, mid)

    def unresolved(lo, hi):
        mid = 0.5 * lo + 0.5 * hi
        inside = jnp.logical_and(mid > lo, mid < hi)
        return jnp.max(jnp.where(inside, 1.0, 0.0))

    def body(carry):
        lo, hi, _ = carry
        for _ in range(SEARCH_UNROLL):
            lo, hi = step(lo, hi)
        return lo, hi, unresolved(lo, hi)

    lo, hi, _ = lax.while_loop(lambda carry: carry[2] > 0.0, body, (lo0, hi0, unresolved(lo0, hi0)))

    score = score_ref[...]
    bias_ref[0] = jnp.where(jnp.logical_and(score >= lo, adm), 0.0, -jnp.inf).astype(bias_ref.dtype)

    tie = jnp.where(jnp.logical_and(searching, count_ge(lo) != kf), 1.0, 0.0)

    @pl.when(jnp.max(tie) > 0.0)
    def _():
        score = score_ref[...]
        t = jnp.where(count_ge(hi) >= kf, hi, lo)
        gt = score > t
        eq = score == t
        need = kf - jnp.sum(jnp.where(gt, 1.0, 0.0), axis=1, keepdims=True)
        r = lax.broadcasted_iota(jnp.int32, (LANES, LANES), 0)
        cidx = lax.broadcasted_iota(jnp.int32, (LANES, LANES), 1)
        tri = jnp.where(r <= cidx, 1.0, 0.0).astype(BF16)
        carry = jnp.zeros((tq, 1), F32)
        for s in range(n_keys // LANES):
            sl = slice(s * LANES, (s + 1) * LANES)
            e = jnp.where(eq[:, sl], 1.0, 0.0)
            pre = jnp.dot(e.astype(BF16), tri, preferred_element_type=F32) + carry
            carry = pre[:, LANES - 1:LANES]
            keep = jnp.logical_or(gt[:, sl], jnp.logical_and(eq[:, sl], pre <= need))
            plain = jnp.logical_and(score[:, sl] >= lo, adm[:, sl])
            bias = jnp.where(tie > 0.0, jnp.where(keep, 0.0, -jnp.inf), jnp.where(plain, 0.0, -jnp.inf))
            bias_ref[0, :, sl] = bias.astype(bias_ref.dtype)


def _select(qi, wi, klo, khi, tq, q_start, n_q, n_keys, pos0, topk):
    b, _, dq = qi.shape
    off = q_start // tq
    return pl.pallas_call(
        functools.partial(_select_kernel, tq=tq, n_keys=n_keys, qpos0=pos0 + q_start, topk=topk),
        grid=(b, n_q // tq),
        in_specs=[pl.BlockSpec((1, tq, dq), lambda i, j: (i, off + j, 0)),
                  pl.BlockSpec((1, tq, IDX_HEADS), lambda i, j: (i, off + j, 0)),
                  pl.BlockSpec((1, LANES, n_keys), lambda i, j: (i, 0, 0)),
                  pl.BlockSpec((1, LANES, n_keys), lambda i, j: (i, 0, 0))],
        out_specs=pl.BlockSpec((1, tq, n_keys), lambda i, j: (i, j, 0)),
        out_shape=jax.ShapeDtypeStruct((b, n_q, n_keys), BF16),
        scratch_shapes=[pltpu.VMEM((tq, n_keys), F32)],
        compiler_params=_params("parallel", "arbitrary"),
        name="select_topk",
    )(qi, wi, klo, khi)


def _attn_kernel(q_ref, bias_ref, *refs, heads, seg_lens, head_major):
    nseg = len(seg_lens)
    k_refs, v_refs, o_ref = refs[:nseg], refs[nseg:2 * nseg], refs[-1]

    def slab(ref, h, n, strided):
        if strided:
            return ref.reshape(n * heads, HEAD_DIM)[pl.ds(h, n, stride=heads), :].astype(BF16)
        return ref[0, :, h * HEAD_DIM:(h + 1) * HEAD_DIM]

    offs = [sum(seg_lens[:i]) for i in range(nseg)]
    biases = [bias_ref[0, :, o:o + n].astype(F32) for o, n in zip(offs, seg_lens)]
    for h in range(heads):
        sl = slice(h * HEAD_DIM, (h + 1) * HEAD_DIM)
        qh = q_ref[0, :, sl]
        scores = []
        for k_ref, bias, n, hm in zip(k_refs, biases, seg_lens, head_major):
            scores.append(lax.dot_general(qh, slab(k_ref, h, n, hm), (((1,), (1,)), ((), ())),
                                          preferred_element_type=F32) + bias)
        m = scores[0].max(axis=-1, keepdims=True)
        for s in scores[1:]:
            m = jnp.maximum(m, s.max(axis=-1, keepdims=True))
        denom = None
        out = None
        for s, v_ref, n, hm in zip(scores, v_refs, seg_lens, head_major):
            p = jnp.exp2(s - m).astype(BF16)
            if hm:
                psum = p.astype(F32).sum(axis=-1, keepdims=True)
                pv = jnp.dot(p, slab(v_ref, h, n, hm), preferred_element_type=F32)
            else:
                pv2 = jnp.dot(p, v_ref[0, :, 2 * h * HEAD_DIM:2 * (h + 1) * HEAD_DIM],
                              preferred_element_type=F32)
                pv, psum = pv2[:, :HEAD_DIM], pv2[:, HEAD_DIM:]
            denom = psum if denom is None else denom + psum
            out = pv if out is None else out + pv
        o_ref[0, :, sl] = (out / denom).astype(o_ref.dtype)


def _attention(q, bias, segs, q_start, heads_per_step, layer=0, out_prev=None):
    b, t, d = q.shape
    n_q = bias.shape[1]
    qb = q_start // n_q
    dh = heads_per_step * HEAD_DIM
    seg_lens = tuple(n for _, _, n in segs)
    head_major = tuple(k.ndim == 5 for k, _, _ in segs)
    assert not any(head_major) or heads_per_step == SUBLANES
    cache_spec = lambda n: pl.BlockSpec((None, None, n, heads_per_step, HEAD_DIM),
                                        lambda i, g: (layer, i, 0, g, 0))
    k_specs = [cache_spec(n) if hm else pl.BlockSpec((1, n, dh), lambda i, g: (i, 0, g))
               for n, hm in zip(seg_lens, head_major)]
    v_specs = [cache_spec(n) if hm else pl.BlockSpec((1, n, 2 * dh), lambda i, g: (i, 0, g))
               for n, hm in zip(seg_lens, head_major)]
    in_specs = [pl.BlockSpec((1, n_q, dh), lambda i, g: (i, qb, g)),
                pl.BlockSpec((1, n_q, sum(seg_lens)), lambda i, g: (i, 0, 0))] + k_specs + v_specs
    args = [q, bias] + [k for k, _, _ in segs] + [v for _, v, _ in segs]
    aliases = {}
    if out_prev is not None:
        in_specs.append(pl.BlockSpec(memory_space=pl.ANY))
        aliases = {len(args): 0}
        args.append(out_prev)
    return pl.pallas_call(
        functools.partial(_attn_kernel, heads=heads_per_step, seg_lens=seg_lens, head_major=head_major),
        grid=(b, d // dh),
        in_specs=in_specs,
        out_specs=pl.BlockSpec((1, n_q, dh), lambda i, g: (i, qb, g)),
        out_shape=jax.ShapeDtypeStruct((b, t, d), BF16),
        input_output_aliases=aliases,
        compiler_params=_params("parallel", "arbitrary"),
        name="sparse_attention",
    )(*args)


def _postmix_kernel(x_ref, c_ref, a_ref, sg_ref, wc_ref, wa_ref, wo_ref, o_ref):
    d = x_ref.shape[1]
    c = jnp.dot(c_ref[...], wc_ref[...], preferred_element_type=F32)
    a = jnp.dot(a_ref[...], wa_ref[...], preferred_element_type=F32)
    m = sg_ref[:, :d].astype(F32) * c + sg_ref[:, d:].astype(F32) * a
    o_ref[...] = x_ref[...] + jnp.dot(m.astype(BF16), wo_ref[...], preferred_element_type=F32)


def _postmix(x, cact, attn, sg, wc, wa, wo, tm=256):
    m, d = x.shape
    tm = min(tm, m)
    row = lambda width: pl.BlockSpec((tm, width), lambda i: (i, 0))
    full = lambda w: pl.BlockSpec(w.shape, lambda i: (0, 0), pipeline_mode=pl.Buffered(1))
    return pl.pallas_call(
        _postmix_kernel,
        grid=(m // tm,),
        in_specs=[row(d), row(cact.shape[1]), row(attn.shape[1]), row(sg.shape[1]),
                  full(wc), full(wa), full(wo)],
        out_specs=row(d),
        out_shape=jax.ShapeDtypeStruct((m, d), F32),
        compiler_params=_params("parallel"),
        name="postmix",
    )(x, cact, attn, sg, wc, wa, wo)


def _ffn_kernel(x_ref, g_ref, wg_ref, wu_ref, wd_ref, gf_ref, o_ref, h_ref):
    f = pl.program_id(1)

    @pl.when(f == 0)
    def _():
        x = x_ref[...]
        ms = jnp.mean(x * x, axis=-1, keepdims=True)
        h_ref[...] = (x * lax.rsqrt(ms + EPS) * g_ref[...]).astype(h_ref.dtype)
        o_ref[...] = jnp.zeros_like(o_ref)

    h = h_ref[...]
    gate = jnp.dot(h, wg_ref[...], preferred_element_type=F32)
    up = jnp.dot(h, wu_ref[...], preferred_element_type=F32)
    act = (gate * jax.nn.sigmoid(gate) * up).astype(BF16)
    o_ref[...] += jnp.dot(act, wd_ref[...], preferred_element_type=F32)

    @pl.when(f == pl.num_programs(1) - 1)
    def _():
        y = x_ref[...] + o_ref[...]
        ms = jnp.mean(y * y, axis=-1, keepdims=True)
        o_ref[...] = y * lax.rsqrt(ms + EPS) * gf_ref[...]


def _ffn(x, g_ffn, wg, wu, wd, g_final, tm=512):
    m, d = x.shape
    dff = wg.shape[1]
    tf = min(FFN_TF, dff)
    tm = min(tm, m)
    return pl.pallas_call(
        _ffn_kernel,
        grid=(m // tm, dff // tf),
        in_specs=[pl.BlockSpec((tm, d), lambda i, f: (i, 0)),
                  pl.BlockSpec((1, d), lambda i, f: (0, 0)),
                  pl.BlockSpec((d, tf), lambda i, f: (0, f)),
                  pl.BlockSpec((d, tf), lambda i, f: (0, f)),
                  pl.BlockSpec((tf, d), lambda i, f: (f, 0)),
                  pl.BlockSpec((1, d), lambda i, f: (0, 0))],
        out_specs=pl.BlockSpec((tm, d), lambda i, f: (i, 0)),
        out_shape=jax.ShapeDtypeStruct((m, d), F32),
        scratch_shapes=[pltpu.VMEM((tm, d), BF16)],
        compiler_params=_params("parallel", "arbitrary"),
        name="ffn",
    )(x, g_ffn.reshape(1, d), wg, wu, wd, g_final.reshape(1, d))


def _prep_weights(w_in, d_conv, d_attn):
    s = [d_conv, d_conv, d_attn, d_attn, d_attn, IDX_HEADS * IDX_DIM, IDX_DIM, IDX_HEADS]
    pts = [0]
    for n in s:
        pts.append(pts[-1] + n)
    wa, wb, wq, wk, wv, wqi, wki, wwi = [w_in[:, pts[i]:pts[i + 1]] for i in range(8)]
    wgate = w_in[:, pts[8]:]
    d = w_in.shape[0]
    tn_glu = min(PROJ_TN, 2 * d_conv)
    half = tn_glu // 2
    glu = jnp.concatenate([wa.reshape(d, -1, half), wb.reshape(d, -1, half)], axis=2).reshape(d, 2 * d_conv)
    pad = jnp.zeros((d, LANES - IDX_DIM - IDX_HEADS), w_in.dtype)
    idx = jnp.concatenate([wqi, wki, wwi, pad], axis=1)
    cast = lambda w: w.astype(BF16)
    return dict(glu=cast(glu), tn_glu=tn_glu, q=cast(wq), k=cast(wk), v=cast(wv), gate=cast(wgate),
                idx=cast(idx))


def _group(x, past_conv, past_k, past_v, past_ki, layer, wts, p, tq, q_class, conv_chunk, heads_per_step):
    b, t, d = x.shape
    q_class = min(q_class, t)
    m = b * t
    x2 = x.reshape(m, d)
    d_conv = p["conv_w"].shape[1]
    d_attn = N_HEADS * HEAD_DIM
    h = _rmsnorm(x2, p["g_mix"], BF16)
    tn_glu = wts["tn_glu"]
    (u,) = _proj(h, wts["glu"], tn_glu, _epi_glu, [((tn_glu // 2,), F32)], name="proj_glu")
    (q,) = _proj(h, wts["q"], PROJ_TN, _epi_query, [((PROJ_TN,), BF16)], name="proj_q")
    hp = PROJ_TN // HEAD_DIM
    k32, kbf = _proj(h, wts["k"], PROJ_TN, _epi_key, [((hp, HEAD_DIM), F32), ((PROJ_TN,), BF16)],
                     name="proj_k")
    v32, vbf = _proj(h, wts["v"], PROJ_TN, _epi_value, [((hp, HEAD_DIM), F32), ((2 * PROJ_TN,), BF16)],
                     name="proj_v")
    tn_gate = min(PROJ_TN, wts["gate"].shape[1])
    (sg,) = _proj(h, wts["gate"], tn_gate, _epi_sigmoid, [((tn_gate,), BF16)], name="proj_gates")
    qi, ki, wi = _proj(
        h, wts["idx"], wts["idx"].shape[1], _epi_indexer,
        [((IDX_HEADS * IDX_DIM,), BF16), ((IDX_DIM,), F32), ((IDX_HEADS,), F32)],
        extra=(p["idx_ln_g"].reshape(1, IDX_DIM), p["idx_ln_b"].reshape(1, IDX_DIM)),
        name="proj_indexer")

    u3 = u.reshape(b, t, d_conv)
    ki3 = ki.reshape(b, t, IDX_DIM)
    if past_conv is None:
        cache_pad = jnp.zeros((b, CONV_HALO, d_conv), F32)
        ki_all = ki3
        n_past = 0
    else:
        cache_pad = jnp.pad(past_conv.astype(F32), ((0, 0), (CONV_HALO - (CONV_WIDTH - 1), 0), (0, 0)))
        n_past = past_k.shape[2]
        ki_all = jnp.concatenate([past_ki, ki3], axis=1)
    cact = _conv_branch(u3, cache_pad, p["conv_w"], p["conv_b"], p["conv_ln_g"], p["conv_ln_b"], conv_chunk)

    n_real = n_past + t
    n_keys = -(-n_real // LANES) * LANES
    topk = min(TOPK_MAX, n_real // 4)
    kit = jnp.swapaxes(jnp.pad(ki_all, ((0, 0), (0, n_keys - n_real), (0, 0))), 1, 2).astype(BF16)
    zeros = jnp.zeros_like(kit)
    klo = jnp.concatenate([kit, zeros], axis=1)
    khi = jnp.concatenate([zeros, kit], axis=1)
    qi3 = qi.reshape(b, t, -1)
    wi3 = wi.reshape(b, t, IDX_HEADS)
    q3 = q.reshape(b, t, d_attn)
    kb3 = kbf.reshape(b, t, d_attn)
    vb3 = vbf.reshape(b, t, 2 * d_attn)
    if past_k is None:
        attn = jnp.zeros((b, t, d_attn), BF16)
        for q_start in range(0, t, q_class):
            n_k = q_start + q_class
            bias = _select(qi3, wi3, klo, khi, tq, q_start, q_class, n_k, 0, topk)
            attn = _attention(q3, bias, [(kb3, vb3, n_k)], q_start, heads_per_step, out_prev=attn)
    else:
        bias = _select(qi3, wi3, klo, khi, tq, 0, t, n_keys, n_past, topk)
        pad = ((0, 0), (0, n_keys - n_real), (0, 0))
        segs = [(past_k, past_v, n_past), (jnp.pad(kb3, pad), jnp.pad(vb3, pad), n_keys - n_past)]
        attn = _attention(q3, bias, segs, 0, heads_per_step, layer=layer)

    x1 = _postmix(x2, cact.reshape(m, d_conv), attn.reshape(m, d_attn), sg,
                  p["w_conv_out"], p["w_attn_out"], p["w_out"])
    tail = CONV_WIDTH - 1
    return (x1, u3[:, t - tail:, :], k32.reshape(b, t, N_HEADS, HEAD_DIM),
            v32.reshape(b, t, N_HEADS, HEAD_DIM), ki3)


def kernel(x_prompt, x_sample, cache_conv, cache_k, cache_v, cache_kidx, g_mix, w_in, conv_w, conv_b,
           conv_ln_g, conv_ln_b, w_conv_out, idx_ln_g, idx_ln_b, w_attn_out, w_out, g_ffn, w_gate, w_up,
           w_down, g_final):
    depth = w_in.shape[0]
    assert depth == 1 and x_prompt.shape[1] >= CONV_WIDTH - 1 and x_sample.shape[1] >= CONV_WIDTH - 1
    d_conv = conv_w.shape[2]
    d_attn = w_attn_out.shape[1]
    l = 0
    wts = _prep_weights(w_in[l], d_conv, d_attn)
    p = dict(g_mix=g_mix[l], conv_w=conv_w[l], conv_b=conv_b[l], conv_ln_g=conv_ln_g[l],
             conv_ln_b=conv_ln_b[l], idx_ln_g=idx_ln_g[l], idx_ln_b=idx_ln_b[l],
             w_conv_out=w_conv_out[l].astype(BF16), w_attn_out=w_attn_out[l].astype(BF16),
             w_out=w_out[l].astype(BF16))
    wg, wu, wd = w_gate[l].astype(BF16), w_up[l].astype(BF16), w_down[l].astype(BF16)

    xp, conv_p, k_p, v_p, kidx_p = _group(
        x_prompt, None, None, None, None, l, wts, p, tq=128, q_class=512, conv_chunk=256,
        heads_per_step=8)
    xs, conv_s, k_s, v_s, kidx_s = _group(
        x_sample, cache_conv[l], cache_k, cache_v, cache_kidx[l], l, wts, p,
        tq=x_sample.shape[1], q_class=x_sample.shape[1], conv_chunk=x_sample.shape[1],
        heads_per_step=8)

    y_prompt = _ffn(xp, g_ffn[l], wg, wu, wd, g_final).reshape(x_prompt.shape)
    y_sample = _ffn(xs, g_ffn[l], wg, wu, wd, g_final).reshape(x_sample.shape)
    return (y_prompt, y_sample, conv_p[None], k_p[None], v_p[None], kidx_p[None],
            conv_s[None], k_s[None], v_s[None], kidx_s[None])
```

```python
import functools

import jax
import jax.numpy as jnp
from jax import lax
from jax.experimental import pallas as pl
from jax.experimental.pallas import tpu as pltpu

F32 = jnp.float32
BF16 = jnp.bfloat16

CHUNK = 64
CONV_WIDTH = 31
N_HEADS = 16
HEAD_DIM = 128
IDX_HEADS = 16
IDX_DIM = 64
TOPK_MAX = 256
EPS = 1e-6
LOG2_E = 1.4426950408889634

LANES = 128
SUBLANES = 8
CONV_HALO = 32
CONV_ROWS = 32
SEARCH_UNROLL = 4
PROJ_TN = 1024
FFN_TF = 512
VMEM_LIMIT = 56 * 2**20


def _params(*sem):
    return pltpu.CompilerParams(dimension_semantics=sem, vmem_limit_bytes=VMEM_LIMIT)


def _rms_kernel(x_ref, g_ref, o_ref):
    x = x_ref[...]
    ms = jnp.mean(x * x, axis=-1, keepdims=True)
    o_ref[...] = (x * lax.rsqrt(ms + EPS) * g_ref[...]).astype(o_ref.dtype)


def _rmsnorm(x, g, out_dtype, tm=512):
    m, d = x.shape
    tm = min(tm, m)
    return pl.pallas_call(
        _rms_kernel,
        grid=(m // tm,),
        in_specs=[pl.BlockSpec((tm, d), lambda i: (i, 0)),
                  pl.BlockSpec((1, d), lambda i: (0, 0))],
        out_specs=pl.BlockSpec((tm, d), lambda i: (i, 0)),
        out_shape=jax.ShapeDtypeStruct((m, d), out_dtype),
        compiler_params=_params("parallel"),
        name="rmsnorm",
    )(x, g.reshape(1, d))


def _proj_kernel(h_ref, w_ref, *io_refs, epilogue):
    epilogue(jnp.dot(h_ref[...], w_ref[...], preferred_element_type=F32), *io_refs)


def _epi_glu(z, u_ref):
    half = z.shape[1] // 2
    u_ref[...] = z[:, :half] * jax.nn.sigmoid(z[:, half:])


def _epi_query(z, q_ref):
    q_ref[...] = (z * (HEAD_DIM ** -0.5 * LOG2_E)).astype(q_ref.dtype)


def _epi_sigmoid(z, o_ref):
    o_ref[...] = jax.nn.sigmoid(z).astype(o_ref.dtype)


def _store_head_major(z, o32_ref):
    tm, heads = z.shape[0], o32_ref.shape[1]
    flat = o32_ref.reshape(tm * heads, HEAD_DIM)
    for h in range(heads):
        flat[pl.ds(h, tm, stride=heads), :] = z[:, h * HEAD_DIM:(h + 1) * HEAD_DIM]


def _epi_key(z, o32_ref, obf_ref):
    _store_head_major(z, o32_ref)
    obf_ref[...] = z.astype(obf_ref.dtype)


def _epi_value(z, o32_ref, oext_ref):
    _store_head_major(z, o32_ref)
    ones = jnp.ones((z.shape[0], HEAD_DIM), oext_ref.dtype)
    for h in range(o32_ref.shape[1]):
        oext_ref[:, 2 * h * HEAD_DIM:(2 * h + 1) * HEAD_DIM] = (
            z[:, h * HEAD_DIM:(h + 1) * HEAD_DIM].astype(oext_ref.dtype))
        oext_ref[:, (2 * h + 1) * HEAD_DIM:(2 * h + 2) * HEAD_DIM] = ones


def _epi_indexer(z, g_ref, b_ref, qi_ref, ki_ref, wi_ref):
    nq = qi_ref.shape[1]
    qi_ref[...] = z[:, :nq].astype(qi_ref.dtype)
    kz = z[:, nq:nq + IDX_DIM]
    mu = jnp.mean(kz, axis=-1, keepdims=True)
    var = jnp.mean(jnp.square(kz - mu), axis=-1, keepdims=True)
    ki_ref[...] = (kz - mu) * lax.rsqrt(var + EPS) * g_ref[...] + b_ref[...]
    wi_ref[...] = z[:, nq + IDX_DIM:nq + IDX_DIM + IDX_HEADS]


def _proj(h, w, tn, epilogue, outs, extra=(), tm=1024, name="proj"):
    m, k = h.shape
    nt = w.shape[1] // tn
    tm = min(tm, m)

    def out_spec(block_tail):
        pad = (0,) * (len(block_tail) - 1)
        return pl.BlockSpec((tm,) + block_tail, lambda i, j: (i, j) + pad)

    in_specs = [pl.BlockSpec((tm, k), lambda i, j: (i, 0)),
                pl.BlockSpec((k, tn), lambda i, j: (0, j))]
    in_specs += [pl.BlockSpec(e.shape, lambda i, j: (0, 0)) for e in extra]
    return pl.pallas_call(
        functools.partial(_proj_kernel, epilogue=epilogue),
        grid=(m // tm, nt),
        in_specs=in_specs,
        out_specs=[out_spec(bt) for bt, _ in outs],
        out_shape=[jax.ShapeDtypeStruct((m, nt * bt[0]) + bt[1:], dt) for bt, dt in outs],
        compiler_params=_params("parallel", "arbitrary"),
        name=name,
    )(h, w, *extra)


def _conv_kernel(cur_ref, prev_ref, cache_ref, cw_ref, cb_ref, g_ref, b_ref, o_ref, ext_ref, *, tc):
    c = pl.program_id(1)
    rows = CONV_HALO + tc
    ext_ref[0, 0:CONV_HALO, :] = jnp.where(c == 0, cache_ref[0], prev_ref[0])
    ext_ref[0, CONV_HALO:rows, :] = cur_ref[0]
    for r in range(1, SUBLANES):
        ext_ref[r, 0:rows - SUBLANES, :] = ext_ref[0, r:r + rows - SUBLANES, :]
    shift = CONV_HALO - (CONV_WIDTH - 1)

    for base in range(0, tc, CONV_ROWS):
        acc = jnp.broadcast_to(cb_ref[...], (CONV_ROWS, cb_ref.shape[1]))
        for w in range(CONV_WIDTH):
            r = (shift + w) % SUBLANES
            start = base + shift + w - r
            acc = acc + ext_ref[r, start:start + CONV_ROWS, :] * cw_ref[w:w + 1, :]
        mu = jnp.mean(acc, axis=-1, keepdims=True)
        var = jnp.mean(jnp.square(acc - mu), axis=-1, keepdims=True)
        y = (acc - mu) * lax.rsqrt(var + EPS) * g_ref[...] + b_ref[...]
        o_ref[0, base:base + CONV_ROWS, :] = (y * jax.nn.sigmoid(y)).astype(o_ref.dtype)


def _conv_branch(u, cache_pad, conv_w, conv_b, ln_g, ln_b, tc):
    b, t, c = u.shape
    tc = min(tc, t)
    per = tc // CONV_HALO
    return pl.pallas_call(
        functools.partial(_conv_kernel, tc=tc),
        grid=(b, t // tc),
        in_specs=[pl.BlockSpec((1, tc, c), lambda i, j: (i, j, 0)),
                  pl.BlockSpec((1, CONV_HALO, c), lambda i, j: (i, jnp.maximum(j * per - 1, 0), 0)),
                  pl.BlockSpec((1, CONV_HALO, c), lambda i, j: (i, 0, 0)),
                  pl.BlockSpec((CONV_WIDTH, c), lambda i, j: (0, 0)),
                  pl.BlockSpec((1, c), lambda i, j: (0, 0)),
                  pl.BlockSpec((1, c), lambda i, j: (0, 0)),
                  pl.BlockSpec((1, c), lambda i, j: (0, 0))],
        out_specs=pl.BlockSpec((1, tc, c), lambda i, j: (i, j, 0)),
        out_shape=jax.ShapeDtypeStruct((b, t, c), BF16),
        scratch_shapes=[pltpu.VMEM((SUBLANES, CONV_HALO + tc, c), F32)],
        compiler_params=_params("parallel", "arbitrary"),
        name="conv_branch",
    )(u, u, cache_pad, conv_w, conv_b.reshape(1, c), ln_g.reshape(1, c), ln_b.reshape(1, c))


def _select_kernel(qi_ref, wi_ref, klo_ref, khi_ref, bias_ref, score_ref, *, tq, n_keys, qpos0, topk):
    j = pl.program_id(1)
    wi = wi_ref[0] * (IDX_HEADS ** -0.5 * IDX_DIM ** -0.5)
    acc = None
    for p in range(IDX_HEADS // 2):
        qt = qi_ref[0, :, p * LANES:(p + 1) * LANES]
        for head, k_ref in ((2 * p, klo_ref), (2 * p + 1, khi_ref)):
            logits = jnp.dot(qt, k_ref[0], preferred_element_type=F32)
            term = wi[:, head:head + 1] * jnp.maximum(logits, 0.0)
            acc = term if acc is None else acc + term

    kpos = lax.broadcasted_iota(jnp.int32, (tq, n_keys), 1)
    qpos = qpos0 + j * tq + lax.broadcasted_iota(jnp.int32, (tq, 1), 0)
    limit = (qpos // CHUNK + 1) * CHUNK
    adm = kpos < limit
    score_ref[...] = jnp.where(adm, acc, -jnp.inf)
    kf = float(topk)
    searching = jnp.minimum(limit, n_keys) > topk
    lo0 = jnp.min(jnp.where(adm, acc, jnp.inf), axis=1, keepdims=True)
    hi0 = jnp.max(score_ref[...], axis=1, keepdims=True)
    lo0 = jnp.where(searching, lo0, -jnp.inf)
    hi0 = jnp.where(searching, hi0, -jnp.inf)

    def count_ge(t):
        return jnp.sum(jnp.where(score_ref[...] >= t, 1.0, 0.0), axis=1, keepdims=True)

    def step(lo, hi):
        mid = 0.5 * lo + 0.5 * hi
        cnt = count_ge(mid)
        return jnp.where(cnt >= kf, mid, lo), jnp.where(cnt > kf, hi, mid)

    def unresolved(lo, hi):
        mid = 0.5 * lo + 0.5 * hi
        inside = jnp.logical_and(mid > lo, mid < hi)
        return jnp.max(jnp.where(inside, 1.0, 0.0))

    def body(carry):
        lo, hi, _ = carry
        for _ in range(SEARCH_UNROLL):
            lo, hi = step(lo, hi)
        return lo, hi, unresolved(lo, hi)

    lo, hi, _ = lax.while_loop(lambda carry: carry[2] > 0.0, body, (lo0, hi0, unresolved(lo0, hi0)))

    score = score_ref[...]
    bias_ref[0] = jnp.where(jnp.logical_and(score >= lo, adm), 0.0, -jnp.inf).astype(bias_ref.dtype)

    tie = jnp.where(jnp.logical_and(searching, count_ge(lo) != kf), 1.0, 0.0)

    @pl.when(jnp.max(tie) > 0.0)
    def _():
        score = score_ref[...]
        t = jnp.where(count_ge(hi) >= kf, hi, lo)
        gt = score > t
        eq = score == t
        need = kf - jnp.sum(jnp.where(gt, 1.0, 0.0), axis=1, keepdims=True)
        r = lax.broadcasted_iota(jnp.int32, (LANES, LANES), 0)
        cidx = lax.broadcasted_iota(jnp.int32, (LANES, LANES), 1)
        tri = jnp.where(r <= cidx, 1.0, 0.0).astype(BF16)
        carry = jnp.zeros((tq, 1), F32)
        for s in range(n_keys // LANES):
            sl = slice(s * LANES, (s + 1) * LANES)
            e = jnp.where(eq[:, sl], 1.0, 0.0)
            pre = jnp.dot(e.astype(BF16), tri, preferred_element_type=F32) + carry
            carry = pre[:, LANES - 1:LANES]
            keep = jnp.logical_or(gt[:, sl], jnp.logical_and(eq[:, sl], pre <= need))
            plain = jnp.logical_and(score[:, sl] >= lo, adm[:, sl])
            bias = jnp.where(tie > 0.0, jnp.where(keep, 0.0, -jnp.inf), jnp.where(plain, 0.0, -jnp.inf))
            bias_ref[0, :, sl] = bias.astype(bias_ref.dtype)


def _select(qi, wi, klo, khi, tq, q_start, n_q, n_keys, pos0, topk):
    b, _, dq = qi.shape
    off = q_start // tq
    return pl.pallas_call(
        functools.partial(_select_kernel, tq=tq, n_keys=n_keys, qpos0=pos0 + q_start, topk=topk),
        grid=(b, n_q // tq),
        in_specs=[pl.BlockSpec((1, tq, dq), lambda i, j: (i, off + j, 0)),
                  pl.BlockSpec((1, tq, IDX_HEADS), lambda i, j: (i, off + j, 0)),
                  pl.BlockSpec((1, LANES, n_keys), lambda i, j: (i, 0, 0)),
                  pl.BlockSpec((1, LANES, n_keys), lambda i, j: (i, 0, 0))],
        out_specs=pl.BlockSpec((1, tq, n_keys), lambda i, j: (i, j, 0)),
        out_shape=jax.ShapeDtypeStruct((b, n_q, n_keys), BF16),
        scratch_shapes=[pltpu.VMEM((tq, n_keys), F32)],
        compiler_params=_params("parallel", "arbitrary"),
        name="select_topk",
    )(qi, wi, klo, khi)


def _attn_kernel(q_ref, bias_ref, *refs, heads, seg_lens, head_major):
    nseg = len(seg_lens)
    k_refs, v_refs, o_ref = refs[:nseg], refs[nseg:2 * nseg], refs[-1]

    def slab(ref, h, n, strided):
        if strided:
            return ref.reshape(n * heads, HEAD_DIM)[pl.ds(h, n, stride=heads), :].astype(BF16)
        return ref[0, :, h * HEAD_DIM:(h + 1) * HEAD_DIM]

    offs = [sum(seg_lens[:i]) for i in range(nseg)]
    biases = [bias_ref.at[0, :, o:o + n] for o, n in zip(offs, seg_lens)]
    for h in range(heads):
        sl = slice(h * HEAD_DIM, (h + 1) * HEAD_DIM)
        qh = q_ref[0, :, sl]
        scores = []
        for k_ref, bias, n, hm in zip(k_refs, biases, seg_lens, head_major):
            scores.append(lax.dot_general(qh, slab(k_ref, h, n, hm), (((1,), (1,)), ((), ())),
                                          preferred_element_type=F32) + bias[...].astype(F32))
        m = scores[0].max(axis=-1, keepdims=True)
        for s in scores[1:]:
            m = jnp.maximum(m, s.max(axis=-1, keepdims=True))
        denom = None
        out = None
        for s, v_ref, n, hm in zip(scores, v_refs, seg_lens, head_major):
            p = jnp.exp2(s - m).astype(BF16)
            if hm:
                psum = p.astype(F32).sum(axis=-1, keepdims=True)
                pv = jnp.dot(p, slab(v_ref, h, n, hm), preferred_element_type=F32)
            else:
                pv2 = jnp.dot(p, v_ref[0, :, 2 * h * HEAD_DIM:2 * (h + 1) * HEAD_DIM],
                              preferred_element_type=F32)
                pv, psum = pv2[:, :HEAD_DIM], pv2[:, HEAD_DIM:]
            denom = psum if denom is None else denom + psum
            out = pv if out is None else out + pv
        o_ref[0, :, sl] = (out / denom).astype(o_ref.dtype)


def _attention(q, bias, segs, q_start, heads_per_step, layer=0, out_prev=None):
    b, t, d = q.shape
    n_q = bias.shape[1]
    qb = q_start // n_q
    dh = heads_per_step * HEAD_DIM
    seg_lens = tuple(n for _, _, n in segs)
    head_major = tuple(k.ndim == 5 for k, _, _ in segs)
    assert not any(head_major) or heads_per_step == SUBLANES
    cache_spec = lambda n: pl.BlockSpec((None, None, n, heads_per_step, HEAD_DIM),
                                        lambda i, g: (layer, i, 0, g, 0))
    k_specs = [cache_spec(n) if hm else pl.BlockSpec((1, n, dh), lambda i, g: (i, 0, g))
               for n, hm in zip(seg_lens, head_major)]
    v_specs = [cache_spec(n) if hm else pl.BlockSpec((1, n, 2 * dh), lambda i, g: (i, 0, g))
               for n, hm in zip(seg_lens, head_major)]
    in_specs = [pl.BlockSpec((1, n_q, dh), lambda i, g: (i, qb, g)),
                pl.BlockSpec((1, n_q, sum(seg_lens)), lambda i, g: (i, 0, 0))] + k_specs + v_specs
    args = [q, bias] + [k for k, _, _ in segs] + [v for _, v, _ in segs]
    aliases = {}
    if out_prev is not None:
        in_specs.append(pl.BlockSpec(memory_space=pl.ANY))
        aliases = {len(args): 0}
        args.append(out_prev)
    return pl.pallas_call(
        functools.partial(_attn_kernel, heads=heads_per_step, seg_lens=seg_lens, head_major=head_major),
        grid=(b, d // dh),
        in_specs=in_specs,
        out_specs=pl.BlockSpec((1, n_q, dh), lambda i, g: (i, qb, g)),
        out_shape=jax.ShapeDtypeStruct((b, t, d), BF16),
        input_output_aliases=aliases,
        compiler_params=_params("parallel", "arbitrary"),
        name="sparse_attention",
    )(*args)


def _postmix_kernel(x_ref, c_ref, a_ref, sg_ref, wc_ref, wa_ref, wo_ref, o_ref):
    d = x_ref.shape[1]
    c = jnp.dot(c_ref[...], wc_ref[...], preferred_element_type=F32)
    a = jnp.dot(a_ref[...], wa_ref[...], preferred_element_type=F32)
    m = sg_ref[:, :d].astype(F32) * c + sg_ref[:, d:].astype(F32) * a
    o_ref[...] = x_ref[...] + jnp.dot(m.astype(BF16), wo_ref[...], preferred_element_type=F32)


def _postmix(x, cact, attn, sg, wc, wa, wo, tm=256):
    m, d = x.shape
    tm = min(tm, m)
    row = lambda width: pl.BlockSpec((tm, width), lambda i: (i, 0))
    full = lambda w: pl.BlockSpec(w.shape, lambda i: (0, 0), pipeline_mode=pl.Buffered(1))
    return pl.pallas_call(
        _postmix_kernel,
        grid=(m // tm,),
        in_specs=[row(d), row(cact.shape[1]), row(attn.shape[1]), row(sg.shape[1]),
                  full(wc), full(wa), full(wo)],
        out_specs=row(d),
        out_shape=jax.ShapeDtypeStruct((m, d), F32),
        compiler_params=_params("parallel"),
        name="postmix",
    )(x, cact, attn, sg, wc, wa, wo)


def _ffn_kernel(x_ref, g_ref, wg_ref, wu_ref, wd_ref, gf_ref, o_ref, h_ref):
    f = pl.program_id(1)

    @pl.when(f == 0)
    def _():
        x = x_ref[...]
        ms = jnp.mean(x * x, axis=-1, keepdims=True)
        h_ref[...] = (x * lax.rsqrt(ms + EPS) * g_ref[...]).astype(h_ref.dtype)
        o_ref[...] = jnp.zeros_like(o_ref)

    h = h_ref[...]
    gate = jnp.dot(h, wg_ref[...], preferred_element_type=F32)
    up = jnp.dot(h, wu_ref[...], preferred_element_type=F32)
    act = (gate * jax.nn.sigmoid(gate) * up).astype(BF16)
    o_ref[...] += jnp.dot(act, wd_ref[...], preferred_element_type=F32)

    @pl.when(f == pl.num_programs(1) - 1)
    def _():
        y = x_ref[...] + o_ref[...]
        ms = jnp.mean(y * y, axis=-1, keepdims=True)
        o_ref[...] = y * lax.rsqrt(ms + EPS) * gf_ref[...]


def _ffn(x, g_ffn, wg, wu, wd, g_final, tm=512):
    m, d = x.shape
    dff = wg.shape[1]
    tf = min(FFN_TF, dff)
    tm = min(tm, m)
    return pl.pallas_call(
        _ffn_kernel,
        grid=(m // tm, dff // tf),
        in_specs=[pl.BlockSpec((tm, d), lambda i, f: (i, 0)),
                  pl.BlockSpec((1, d), lambda i, f: (0, 0)),
                  pl.BlockSpec((d, tf), lambda i, f: (0, f)),
                  pl.BlockSpec((d, tf), lambda i, f: (0, f)),
                  pl.BlockSpec((tf, d), lambda i, f: (f, 0)),
                  pl.BlockSpec((1, d), lambda i, f: (0, 0))],
        out_specs=pl.BlockSpec((tm, d), lambda i, f: (i, 0)),
        out_shape=jax.ShapeDtypeStruct((m, d), F32),
        scratch_shapes=[pltpu.VMEM((tm, d), BF16)],
        compiler_params=_params("parallel", "arbitrary"),
        name="ffn",
    )(x, g_ffn.reshape(1, d), wg, wu, wd, g_final.reshape(1, d))


def _prep_weights(w_in, d_conv, d_attn):
    s = [d_conv, d_conv, d_attn, d_attn, d_attn, IDX_HEADS * IDX_DIM, IDX_DIM, IDX_HEADS]
    pts = [0]
    for n in s:
        pts.append(pts[-1] + n)
    wa, wb, wq, wk, wv, wqi, wki, wwi = [w_in[:, pts[i]:pts[i + 1]] for i in range(8)]
    wgate = w_in[:, pts[8]:]
    d = w_in.shape[0]
    tn_glu = min(PROJ_TN, 2 * d_conv)
    half = tn_glu // 2
    glu = jnp.concatenate([wa.reshape(d, -1, half), wb.reshape(d, -1, half)], axis=2).reshape(d, 2 * d_conv)
    pad = jnp.zeros((d, LANES - IDX_DIM - IDX_HEADS), w_in.dtype)
    idx = jnp.concatenate([wqi, wki, wwi, pad], axis=1)
    cast = lambda w: w.astype(BF16)
    return dict(glu=cast(glu), tn_glu=tn_glu, q=cast(wq), k=cast(wk), v=cast(wv), gate=cast(wgate),
                idx=cast(idx))


def _group(x, past_conv, past_k, past_v, past_ki, layer, wts, p, tq, q_class, conv_chunk, heads_per_step):
    b, t, d = x.shape
    q_class = min(q_class, t)
    m = b * t
    x2 = x.reshape(m, d)
    d_conv = p["conv_w"].shape[1]
    d_attn = N_HEADS * HEAD_DIM
    h = _rmsnorm(x2, p["g_mix"], BF16)
    tn_glu = wts["tn_glu"]
    (u,) = _proj(h, wts["glu"], tn_glu, _epi_glu, [((tn_glu // 2,), F32)], name="proj_glu")
    (q,) = _proj(h, wts["q"], PROJ_TN, _epi_query, [((PROJ_TN,), BF16)], name="proj_q")
    hp = PROJ_TN // HEAD_DIM
    k32, kbf = _proj(h, wts["k"], PROJ_TN, _epi_key, [((hp, HEAD_DIM), F32), ((PROJ_TN,), BF16)],
                     name="proj_k")
    v32, vbf = _proj(h, wts["v"], PROJ_TN, _epi_value, [((hp, HEAD_DIM), F32), ((2 * PROJ_TN,), BF16)],
                     name="proj_v")
    tn_gate = min(PROJ_TN, wts["gate"].shape[1])
    (sg,) = _proj(h, wts["gate"], tn_gate, _epi_sigmoid, [((tn_gate,), BF16)], name="proj_gates")
    qi, ki, wi = _proj(
        h, wts["idx"], wts["idx"].shape[1], _epi_indexer,
        [((IDX_HEADS * IDX_DIM,), BF16), ((IDX_DIM,), F32), ((IDX_HEADS,), F32)],
        extra=(p["idx_ln_g"].reshape(1, IDX_DIM), p["idx_ln_b"].reshape(1, IDX_DIM)),
        name="proj_indexer")

    u3 = u.reshape(b, t, d_conv)
    ki3 = ki.reshape(b, t, IDX_DIM)
    if past_conv is None:
        cache_pad = jnp.zeros((b, CONV_HALO, d_conv), F32)
        ki_all = ki3
        n_past = 0
    else:
        cache_pad = jnp.pad(past_conv.astype(F32), ((0, 0), (CONV_HALO - (CONV_WIDTH - 1), 0), (0, 0)))
        n_past = past_k.shape[2]
        ki_all = jnp.concatenate([past_ki, ki3], axis=1)
    cact = _conv_branch(u3, cache_pad, p["conv_w"], p["conv_b"], p["conv_ln_g"], p["conv_ln_b"], conv_chunk)

    n_real = n_past + t
    n_keys = -(-n_real // LANES) * LANES
    topk = min(TOPK_MAX, n_real // 4)
    kit = jnp.swapaxes(jnp.pad(ki_all, ((0, 0), (0, n_keys - n_real), (0, 0))), 1, 2).astype(BF16)
    zeros = jnp.zeros_like(kit)
    klo = jnp.concatenate([kit, zeros], axis=1)
    khi = jnp.concatenate([zeros, kit], axis=1)
    qi3 = qi.reshape(b, t, -1)
    wi3 = wi.reshape(b, t, IDX_HEADS)
    q3 = q.reshape(b, t, d_attn)
    kb3 = kbf.reshape(b, t, d_attn)
    vb3 = vbf.reshape(b, t, 2 * d_attn)
    if past_k is None:
        attn = jnp.zeros((b, t, d_attn), BF16)
        for q_start in range(0, t, q_class):
            n_k = q_start + q_class
            bias = _select(qi3, wi3, klo, khi, tq, q_start, q_class, n_k, 0, topk)
            attn = _attention(q3, bias, [(kb3, vb3, n_k)], q_start, heads_per_step, out_prev=attn)
    else:
        bias = _select(qi3, wi3, klo, khi, tq, 0, t, n_keys, n_past, topk)
        pad = ((0, 0), (0, n_keys - n_real), (0, 0))
        segs = [(past_k, past_v, n_past), (jnp.pad(kb3, pad), jnp.pad(vb3, pad), n_keys - n_past)]
        attn = _attention(q3, bias, segs, 0, heads_per_step, layer=layer)

    x1 = _postmix(x2, cact.reshape(m, d_conv), attn.reshape(m, d_attn), sg,
                  p["w_conv_out"], p["w_attn_out"], p["w_out"])
    tail = CONV_WIDTH - 1
    return (x1, u3[:, t - tail:, :], k32.reshape(b, t, N_HEADS, HEAD_DIM),
            v32.reshape(b, t, N_HEADS, HEAD_DIM), ki3)


def kernel(x_prompt, x_sample, cache_conv, cache_k, cache_v, cache_kidx, g_mix, w_in, conv_w, conv_b,
           conv_ln_g, conv_ln_b, w_conv_out, idx_ln_g, idx_ln_b, w_attn_out, w_out, g_ffn, w_gate, w_up,
           w_down, g_final):
    depth = w_in.shape[0]
    assert depth == 1 and x_prompt.shape[1] >= CONV_WIDTH - 1 and x_sample.shape[1] >= CONV_WIDTH - 1
    d_conv = conv_w.shape[2]
    d_attn = w_attn_out.shape[1]
    l = 0
    wts = _prep_weights(w_in[l], d_conv, d_attn)
    p = dict(g_mix=g_mix[l], conv_w=conv_w[l], conv_b=conv_b[l], conv_ln_g=conv_ln_g[l],
             conv_ln_b=conv_ln_b[l], idx_ln_g=idx_ln_g[l], idx_ln_b=idx_ln_b[l],
             w_conv_out=w_conv_out[l].astype(BF16), w_attn_out=w_attn_out[l].astype(BF16),
             w_out=w_out[l].astype(BF16))
    wg, wu, wd = w_gate[l].astype(BF16), w_up[l].astype(BF16), w_down[l].astype(BF16)

    xp, conv_p, k_p, v_p, kidx_p = _group(
        x_prompt, None, None, None, None, l, wts, p, tq=256, q_class=512, conv_chunk=256,
        heads_per_step=8)
    xs, conv_s, k_s, v_s, kidx_s = _group(
        x_sample, cache_conv[l], cache_k, cache_v, cache_kidx[l], l, wts, p,
        tq=x_sample.shape[1], q_class=x_sample.shape[1], conv_chunk=x_sample.shape[1],
        heads_per_step=8)

    y_prompt = _ffn(xp, g_ffn[l], wg, wu, wd, g_final).reshape(x_prompt.shape)
    y_sample = _ffn(xs, g_ffn[l], wg, wu, wd, g_final).reshape(x_sample.shape)
    return (y_prompt, y_sample, conv_p[None], k_p[None], v_p[None], kidx_p[None],
            conv_s[None], k_s[None], v_s[None], kidx_s[None])
```
